```python
import jax, jax.numpy as jnp
from jax import lax
import numpy as np

D_MODEL = 1024
BATCH = 16
SEQ = 2048
DEPTH = 1
DEC_BATCH = 16
DEC_SEQ = 32
PAST_LEN = 2048

CHUNK = 64
D_RNN = 1024
N_RNN_HEADS = 16
RNN_HEAD_DIM = D_RNN // N_RNN_HEADS
CONV_W = 4
LRU_C = 8.0
D_GMLP = 1024
N_GMLP_GROUPS = 8
GMLP_GROUP_DIM = D_GMLP // N_GMLP_GROUPS
MLP_CHUNK = 128
D_FF = ((8 * D_MODEL // 3 + 255) // 256) * 256
D_IN = 2 * D_RNN + 2 * D_GMLP + 2 * D_MODEL
EPS = 1e-6

kernel_name = "hawk_gmlp_parallel_streaming_encoder"


def rmsnorm(x, g):
    xf = x.astype(jnp.float32)
    y = xf * lax.rsqrt(jnp.mean(xf * xf, axis=-1, keepdims=True) + EPS) * g.astype(jnp.float32)
    return y.astype(x.dtype)


def layernorm(x, g, b):
    xf = x.astype(jnp.float32)
    mu = jnp.mean(xf, axis=-1, keepdims=True)
    var = jnp.mean(jnp.square(xf - mu), axis=-1, keepdims=True)
    y = (xf - mu) * lax.rsqrt(var + EPS) * g.astype(jnp.float32) + b.astype(jnp.float32)
    return y.astype(x.dtype)


def causal_conv(x, prev, w, b):
    T = x.shape[1]
    xp = jnp.concatenate([prev.astype(x.dtype), x], axis=1)
    y = b + sum(xp[:, k:k + T] * w[k] for k in range(CONV_W))
    return y, xp[:, -(CONV_W - 1):]


def _lin_combine(left, right):
    a1, b1 = left
    a2, b2 = right
    return a1 * a2, a2 * b1 + b2


def rglru(x, h0, w_a, b_a, w_x, b_x, lam):
    B, T, _ = x.shape
    xh = x.reshape(B, T, N_RNN_HEADS, RNN_HEAD_DIM)
    r = jax.nn.sigmoid(jnp.einsum('bthi,hij->bthj', xh, w_a) + b_a).reshape(B, T, D_RNN)
    i = jax.nn.sigmoid(jnp.einsum('bthi,hij->bthj', xh, w_x) + b_x).reshape(B, T, D_RNN)
    log_a = -LRU_C * r.astype(jnp.float32) * jax.nn.softplus(-lam.astype(jnp.float32))
    a = jnp.exp(log_a)
    mult = jnp.sqrt(-jnp.expm1(2.0 * log_a))
    u = mult * (i * x).astype(jnp.float32)
    u = u.at[:, 0].add(a[:, 0] * h0.astype(jnp.float32))
    _, h = lax.associative_scan(_lin_combine, (a, u), axis=1)
    return h.astype(x.dtype), h[:, -1].astype(x.dtype)


def spatial_gate(v, w_s, b_s):
    B, T, _ = v.shape
    L = min(T, MLP_CHUNK)
    N = T // L
    vh = v.reshape(B, N, L, N_GMLP_GROUPS, GMLP_GROUP_DIM)
    pos = jnp.arange(L)
    mask = (pos[None, :] // CHUNK) <= (pos[:, None] // CHUNK)
    w = jnp.where(mask[None], w_s[:, :L, :L], jnp.zeros((), w_s.dtype))
    out = jnp.einsum('gpq,bnqgd->bnpgd', w, vh) + b_s[:, :L].T[None, None, :, :, None]
    return out.reshape(B, T, D_GMLP)


def layer(x, h0, conv_prev, g_pre_mix, w_in, conv_w, conv_b, w_a, b_a, w_x, b_x, lam,
          w_br_a, ln_g, ln_b, w_s, b_s, w_br_b, w_out, g_post_mix,
          g_pre_ffn, w_ffn_in, w_ffn_out, g_post_ffn):
    xn = rmsnorm(x, g_pre_mix)
    z = xn @ w_in
    xa, ga, u, v, gates = jnp.split(
        z, [D_RNN, 2 * D_RNN, 2 * D_RNN + D_GMLP, 2 * D_RNN + 2 * D_GMLP], axis=-1)
    xc, conv_new = causal_conv(xa, conv_prev, conv_w, conv_b)
    hseq, h_last = rglru(xc, h0, w_a, b_a, w_x, b_x, lam)
    o_a = (hseq * jax.nn.gelu(ga)) @ w_br_a
    vn = layernorm(jax.nn.gelu(v), ln_g, ln_b)
    o_b = (jax.nn.gelu(u) * spatial_gate(vn, w_s, b_s)) @ w_br_b
    g_a, g_b = jnp.split(jax.nn.sigmoid(gates), 2, axis=-1)
    mix = (g_a * o_a + g_b * o_b) @ w_out
    x = x + rmsnorm(mix, g_post_mix)
    hn = rmsnorm(x, g_pre_ffn)
    gate, up = jnp.split(hn @ w_ffn_in, 2, axis=-1)
    f = (jax.nn.silu(gate) * up) @ w_ffn_out
    x = x + rmsnorm(f, g_post_ffn)
    return x, h_last, conv_new, vn


def setup_inputs(seed: int = 0) -> dict:
    key = jax.random.key(seed)
    ks = jax.random.split(key, 32)
    f32 = jnp.float32

    def nrm(k, shape, scale):
        return jax.random.normal(k, shape, f32) * scale

    def gain(k, shape):
        return 1.0 + 0.05 * jax.random.normal(k, shape, f32)

    a0 = jax.random.uniform(ks[11], (DEPTH, D_RNN), f32, 0.9, 0.999)
    sp = -jnp.log(a0) / LRU_C
    lam = -jnp.log(jnp.expm1(sp))
    return {
        "x_prompt": nrm(ks[0], (BATCH, SEQ, D_MODEL), 1.0),
        "x_sample": nrm(ks[1], (DEC_BATCH, DEC_SEQ, D_MODEL), 1.0),
        "state_rglru_h": nrm(ks[2], (DEPTH, DEC_BATCH, D_RNN), 0.5),
        "state_rglru_conv": nrm(ks[3], (DEPTH, DEC_BATCH, CONV_W - 1, D_RNN), 1.0),
        "g_pre_mix": gain(ks[4], (DEPTH, D_MODEL)),
        "w_in": nrm(ks[5], (DEPTH, D_MODEL, D_IN), D_MODEL ** -0.5),
        "conv_w": nrm(ks[6], (DEPTH, CONV_W, D_RNN), CONV_W ** -0.5),
        "conv_b": nrm(ks[7], (DEPTH, D_RNN), 0.01),
        "w_a": nrm(ks[8], (DEPTH, N_RNN_HEADS, RNN_HEAD_DIM, RNN_HEAD_DIM), RNN_HEAD_DIM ** -0.5),
        "b_a": nrm(ks[9], (DEPTH, N_RNN_HEADS, RNN_HEAD_DIM), 0.01),
        "w_x": nrm(ks[10], (DEPTH, N_RNN_HEADS, RNN_HEAD_DIM, RNN_HEAD_DIM), RNN_HEAD_DIM ** -0.5),
        "b_x": nrm(ks[12], (DEPTH, N_RNN_HEADS, RNN_HEAD_DIM), 0.01),
        "lam": lam,
        "w_br_a": nrm(ks[13], (DEPTH, D_RNN, D_MODEL), D_RNN ** -0.5),
        "ln_g": gain(ks[14], (DEPTH, D_GMLP)),
        "ln_b": nrm(ks[15], (DEPTH, D_GMLP), 0.01),
        "w_s": nrm(ks[16], (DEPTH, N_GMLP_GROUPS, MLP_CHUNK, MLP_CHUNK), MLP_CHUNK ** -0.5),
        "b_s": gain(ks[17], (DEPTH, N_GMLP_GROUPS, MLP_CHUNK)),
        "w_br_b": nrm(ks[18], (DEPTH, D_GMLP, D_MODEL), D_GMLP ** -0.5),
        "w_out": nrm(ks[19], (DEPTH, D_MODEL, D_MODEL), D_MODEL ** -0.5),
        "g_post_mix": gain(ks[20], (DEPTH, D_MODEL)),
        "g_pre_ffn": gain(ks[21], (DEPTH, D_MODEL)),
        "w_ffn_in": nrm(ks[22], (DEPTH, D_MODEL, 2 * D_FF), D_MODEL ** -0.5),
        "w_ffn_out": nrm(ks[23], (DEPTH, D_FF, D_MODEL), D_FF ** -0.5),
        "g_post_ffn": gain(ks[24], (DEPTH, D_MODEL)),
    }


def reference(x_prompt, x_sample, state_rglru_h, state_rglru_conv,
              g_pre_mix, w_in, conv_w, conv_b, w_a, b_a, w_x, b_x, lam,
              w_br_a, ln_g, ln_b, w_s, b_s, w_br_b, w_out, g_post_mix,
              g_pre_ffn, w_ffn_in, w_ffn_out, g_post_ffn):
    xp = x_prompt
    xs = x_sample
    bp = x_prompt.shape[0]
    hp_list, cp_list, hs_list, cs_list, vs_list = [], [], [], [], []
    for l in range(DEPTH):
        params = (g_pre_mix[l], w_in[l], conv_w[l], conv_b[l], w_a[l], b_a[l], w_x[l], b_x[l], lam[l],
                  w_br_a[l], ln_g[l], ln_b[l], w_s[l], b_s[l], w_br_b[l], w_out[l], g_post_mix[l],
                  g_pre_ffn[l], w_ffn_in[l], w_ffn_out[l], g_post_ffn[l])
        h0_p = jnp.zeros((bp, D_RNN), xp.dtype)
        c0_p = jnp.zeros((bp, CONV_W - 1, D_RNN), xp.dtype)
        xp, hp, cp, _ = layer(xp, h0_p, c0_p, *params)
        xs, hs, cs, vs = layer(xs, state_rglru_h[l], state_rglru_conv[l], *params)
        hp_list.append(hp)
        cp_list.append(cp)
        hs_list.append(hs)
        cs_list.append(cs)
        vs_list.append(vs)
    new_h_prompt = jnp.stack(hp_list)
    new_conv_prompt = jnp.stack(cp_list)
    new_h_sample = jnp.stack(hs_list)
    new_conv_sample = jnp.stack(cs_list)
    new_v_sample = jnp.stack(vs_list)
    return (xp, xs, new_h_prompt, new_conv_prompt, new_h_sample, new_conv_sample, new_v_sample)
```

```python
import functools

import jax
import jax.numpy as jnp
from jax import lax
from jax.experimental import pallas as pl
from jax.experimental.pallas import tpu as pltpu

D_MODEL = 1024
D_RNN = 1024
N_RNN_HEADS = 16
RNN_HEAD_DIM = D_RNN // N_RNN_HEADS
CONV_W = 4
LRU_C = 8.0
D_GMLP = 1024
N_GMLP_GROUPS = 8
GMLP_GROUP_DIM = D_GMLP // N_GMLP_GROUPS
CHUNK = 64
MLP_CHUNK = 128
D_FF = 2816
EPS = 1e-6

LANES = 128
SUBLANES = 8
MXU_N = 256
N_SLABS = D_RNN // LANES
HEADS_PER_MXU = MXU_N // RNN_HEAD_DIM
N_GATE_GROUPS = D_RNN // MXU_N
ROW_BLOCK = 32
VMEM_LIMIT_BYTES = 56 * 1024 * 1024

BF16 = jnp.bfloat16
F32 = jnp.float32


def _dot(a, b):
    return jnp.dot(a, b, preferred_element_type=F32)


def _rms_scale(v):
    return lax.rsqrt(jnp.mean(v * v, axis=-1, keepdims=True) + EPS)


def _mixer_kernel(x_ref, h0_ref, c0_ref, gpre_ref, win_ref, cw_ref, cb_ref, wg_ref, bg_ref, lam_ref,
                  wbra_ref, lng_ref, lnb_ref, ws_ref, bst_ref, wbrb_ref, wout_ref, gpost_ref,
                  *rest, nb, tt, emit_vn):
    if emit_vn:
        x1_ref, hl_ref, cn_ref, vn_ref = rest[:4]
        scratch = rest[4:]
    else:
        x1_ref, hl_ref, cn_ref = rest[:3]
        vn_ref = None
        scratch = rest[3:]
    xn_scr, buf1, buf2, wide_scr, ya_scr, yb_scr, vnb_scr, m_scr, hcar, ccar = scratch

    m = nb * tt
    pitch = tt + SUBLANES
    bi = pl.program_id(0)
    ti = pl.program_id(1)
    n_rb = tt // ROW_BLOCK

    def seq_rows(t):
        return pl.ds(t, nb, stride=pitch)

    @pl.when(ti == 0)
    def _():
        hcar[...] = h0_ref[bi]
        for k in range(CONV_W - 1):
            ccar[k] = c0_ref[k, bi]

    for s in range(N_SLABS):
        for b in range(nb):
            buf1[s, b * pitch + tt:(b + 1) * pitch, :] = jnp.zeros((SUBLANES, LANES), F32)

    gpre = gpre_ref[...]

    def norm_body(i, carry):
        b = i // n_rb
        r0 = pl.multiple_of((i % n_rb) * ROW_BLOCK, ROW_BLOCK)
        xb = x_ref[b, pl.ds(r0, ROW_BLOCK), :]
        y = xb * _rms_scale(xb) * gpre
        xn_scr[pl.ds(pl.multiple_of(b * tt + r0, ROW_BLOCK), ROW_BLOCK), :] = y.astype(BF16)
        return carry

    lax.fori_loop(0, nb * n_rb, norm_body, 0)

    def in_proj(col0, c):
        lo = col0 + c * MXU_N
        return _dot(xn_scr[...], win_ref[:, lo:lo + MXU_N])

    for c in range(D_RNN // MXU_N):
        xa = in_proj(0, c)
        for b in range(nb):
            for s2 in range(MXU_N // LANES):
                buf1[(MXU_N // LANES) * c + s2, b * pitch:b * pitch + tt, :] = (
                    xa[b * tt:(b + 1) * tt, s2 * LANES:(s2 + 1) * LANES])

    for s in range(N_SLABS):
        ls = slice(s * LANES, (s + 1) * LANES)
        w = [jnp.broadcast_to(cw_ref[k:k + 1, ls], (nb, LANES)) for k in range(CONV_W)]
        bias = jnp.broadcast_to(cb_ref[:, ls], (nb, LANES))

        def conv_body(t, prev, s=s, w=w, bias=bias):
            p3, p2, p1 = prev
            cur = buf1[s, seq_rows(t), :]
            buf1[s, seq_rows(t), :] = bias + w[0] * p3 + w[1] * p2 + w[2] * p1 + w[3] * cur
            return (p2, p1, cur)

        prev = lax.fori_loop(0, tt, conv_body, tuple(ccar[k, :, ls] for k in range(CONV_W - 1)),
                             unroll=8)
        for k in range(CONV_W - 1):
            ccar[k, :, ls] = prev[k]
    for k in range(CONV_W - 1):
        cn_ref[k, bi] = ccar[k]

    lam = lam_ref[...]
    neg_lam = -lam
    softplus_neg_lam = jnp.maximum(neg_lam, 0.0) + jnp.log1p(jnp.exp(-jnp.abs(neg_lam)))
    for j in range(N_GATE_GROUPS):
        s_lo = (MXU_N // LANES) * j
        xc = jnp.concatenate([buf1[s_lo + s2] for s2 in range(MXU_N // LANES)], axis=1)
        pre = _dot(xc.astype(BF16), wg_ref[j])
        cols = slice(j * MXU_N, (j + 1) * MXU_N)
        r = jax.nn.sigmoid(pre[:, :MXU_N] + bg_ref[0:1, cols])
        i_gate = jax.nn.sigmoid(pre[:, MXU_N:] + bg_ref[1:2, cols])
        log_a = (-LRU_C) * r * softplus_neg_lam[:, cols]
        a = jnp.exp(log_a)
        mult = jnp.sqrt(1.0 - a * a)
        u = mult * (i_gate * xc)
        for s2 in range(MXU_N // LANES):
            buf1[s_lo + s2] = a[:, s2 * LANES:(s2 + 1) * LANES]
            buf2[s_lo + s2] = u[:, s2 * LANES:(s2 + 1) * LANES]

    def scan_body(t, hs):
        out = []
        for s in range(N_SLABS):
            h = buf1[s, seq_rows(t), :] * hs[s] + buf2[s, seq_rows(t), :]
            buf2[s, seq_rows(t), :] = h
            out.append(h)
        return tuple(out)

    hs = lax.fori_loop(0, tt, scan_body,
                       tuple(hcar[:, s * LANES:(s + 1) * LANES] for s in range(N_SLABS)), unroll=4)
    for s in range(N_SLABS):
        hcar[:, s * LANES:(s + 1) * LANES] = hs[s]
    hl_ref[bi] = hcar[...]

    for c in range(D_RNN // MXU_N):
        gl = jax.nn.gelu(in_proj(D_RNN, c))
        for b in range(nb):
            for s2 in range(MXU_N // LANES):
                s = (MXU_N // LANES) * c + s2
                ya_scr[b * tt:(b + 1) * tt, s * LANES:(s + 1) * LANES] = (
                    buf2[s, b * pitch:b * pitch + tt, :]
                    * gl[b * tt:(b + 1) * tt, s2 * LANES:(s2 + 1) * LANES]).astype(BF16)

    for c in range(D_GMLP // MXU_N):
        wide_scr[:, c * MXU_N:(c + 1) * MXU_N] = jax.nn.gelu(in_proj(2 * D_RNN + D_GMLP, c))
    lng = lng_ref[...]
    lnb = lnb_ref[...]

    def ln_body(i, carry):
        r0 = pl.multiple_of(i * ROW_BLOCK, ROW_BLOCK)
        gv = wide_scr[pl.ds(r0, ROW_BLOCK), :]
        d = gv - jnp.mean(gv, axis=-1, keepdims=True)
        vn = d * lax.rsqrt(jnp.mean(d * d, axis=-1, keepdims=True) + EPS) * lng + lnb
        vnb_scr[pl.ds(r0, ROW_BLOCK), :] = vn.astype(BF16)
        if emit_vn:
            vn_ref[i // n_rb, pl.ds(pl.multiple_of((i % n_rb) * ROW_BLOCK, ROW_BLOCK), ROW_BLOCK), :] = vn
        return carry

    lax.fori_loop(0, m // ROW_BLOCK, ln_body, 0)

    pos = lax.broadcasted_iota(jnp.int32, (tt, tt), 0) // CHUNK
    qos = lax.broadcasted_iota(jnp.int32, (tt, tt), 1) // CHUNK
    causal = qos <= pos
    for c in range(D_GMLP // MXU_N):
        gu = jax.nn.gelu(in_proj(2 * D_RNN, c))
        for s2 in range(MXU_N // LANES):
            g = (MXU_N // LANES) * c + s2
            gl_cols = slice(g * GMLP_GROUP_DIM, (g + 1) * GMLP_GROUP_DIM)
            wm = jnp.where(causal, ws_ref[g], 0.0).astype(BF16)
            v_g = jnp.concatenate([vnb_scr[b * tt:(b + 1) * tt, gl_cols] for b in range(nb)], axis=1)
            sg = _dot(wm, v_g) + bst_ref[:, g:g + 1]
            for b in range(nb):
                yb_scr[b * tt:(b + 1) * tt, gl_cols] = (
                    gu[b * tt:(b + 1) * tt, s2 * LANES:(s2 + 1) * LANES]
                    * sg[:, b * GMLP_GROUP_DIM:(b + 1) * GMLP_GROUP_DIM]).astype(BF16)

    for c in range(D_MODEL // MXU_N):
        cols = slice(c * MXU_N, (c + 1) * MXU_N)
        g_a = jax.nn.sigmoid(in_proj(2 * D_RNN + 2 * D_GMLP, c))
        g_b = jax.nn.sigmoid(in_proj(2 * D_RNN + 2 * D_GMLP + D_MODEL, c))
        o_a = _dot(ya_scr[...], wbra_ref[:, cols])
        o_b = _dot(yb_scr[...], wbrb_ref[:, cols])
        m_scr[:, cols] = (g_a * o_a + g_b * o_b).astype(BF16)

    for c in range(D_MODEL // MXU_N):
        cols = slice(c * MXU_N, (c + 1) * MXU_N)
        wide_scr[:, cols] = _dot(m_scr[...], wout_ref[:, cols])
    gpost = gpost_ref[...]

    def out_body(i, carry):
        b = i // n_rb
        r0 = pl.multiple_of((i % n_rb) * ROW_BLOCK, ROW_BLOCK)
        mix = wide_scr[pl.ds(pl.multiple_of(i * ROW_BLOCK, ROW_BLOCK), ROW_BLOCK), :]
        x1_ref[b, pl.ds(r0, ROW_BLOCK), :] = (
            x_ref[b, pl.ds(r0, ROW_BLOCK), :] + mix * _rms_scale(mix) * gpost)
        return carry

    lax.fori_loop(0, nb * n_rb, out_body, 0)


def _ffn_kernel(x_ref, gpre_ref, w1_ref, w2_ref, gpost_ref, o_ref, hn_scr, f_scr, wide_scr, *, m):
    gpre = gpre_ref[...]

    def norm_body(i, carry):
        rows = pl.ds(pl.multiple_of(i * ROW_BLOCK, ROW_BLOCK), ROW_BLOCK)
        xb = x_ref[rows, :]
        hn_scr[rows, :] = (xb * _rms_scale(xb) * gpre).astype(BF16)
        return carry

    lax.fori_loop(0, m // ROW_BLOCK, norm_body, 0)

    for c in range(D_FF // MXU_N):
        gate = _dot(hn_scr[...], w1_ref[:, c * MXU_N:(c + 1) * MXU_N])
        up = _dot(hn_scr[...], w1_ref[:, D_FF + c * MXU_N:D_FF + (c + 1) * MXU_N])
        f_scr[:, c * MXU_N:(c + 1) * MXU_N] = (jax.nn.silu(gate) * up).astype(BF16)

    for c in range(D_MODEL // MXU_N):
        cols = slice(c * MXU_N, (c + 1) * MXU_N)
        wide_scr[:, cols] = _dot(f_scr[...], w2_ref[:, cols])
    gpost = gpost_ref[...]

    def out_body(i, carry):
        rows = pl.ds(pl.multiple_of(i * ROW_BLOCK, ROW_BLOCK), ROW_BLOCK)
        f = wide_scr[rows, :]
        o_ref[rows, :] = x_ref[rows, :] + f * _rms_scale(f) * gpost
        return carry

    lax.fori_loop(0, m // ROW_BLOCK, out_body, 0)


def _resident(shape):
    nd = len(shape)
    return pl.BlockSpec(shape, lambda *_: (0,) * nd, pipeline_mode=pl.Buffered(1))


def _mixer_call(x, h0, c0, p, *, nb, tt, emit_vn):
    bsz, t_len, _ = x.shape
    n_t = t_len // tt
    m = nb * tt
    pitch = tt + SUBLANES
    n_b = bsz // nb
    x4 = x.reshape(bsz, n_t, tt, D_MODEL)
    tile = pl.BlockSpec((nb, None, tt, D_MODEL), lambda bi, ti: (bi, ti, 0, 0))
    h0 = h0.reshape(n_b, nb, D_RNN)
    c0 = c0.reshape(CONV_W - 1, n_b, nb, D_RNN)
    small = [h0, c0, p["g_pre_mix"], p["w_in"], p["conv_w"], p["conv_b"], p["w_gate"], p["b_gate"],
             p["lam"], p["w_br_a"], p["ln_g"], p["ln_b"], p["w_s"][:, :tt, :tt], p["b_s_t"][:tt],
             p["w_br_b"], p["w_out"], p["g_post_mix"]]
    out_shape = [jax.ShapeDtypeStruct(x4.shape, F32),
                 jax.ShapeDtypeStruct(h0.shape, F32),
                 jax.ShapeDtypeStruct(c0.shape, F32)]
    out_specs = [tile,
                 pl.BlockSpec(h0.shape, lambda bi, ti: (0, 0, 0)),
                 pl.BlockSpec(c0.shape, lambda bi, ti: (0, 0, 0, 0))]
    if emit_vn:
        out_shape.append(jax.ShapeDtypeStruct(x4.shape, F32))
        out_specs.append(tile)
    scratch = [pltpu.VMEM((m, D_MODEL), BF16),
               pltpu.VMEM((N_SLABS, nb * pitch, LANES), F32),
               pltpu.VMEM((N_SLABS, nb * pitch, LANES), F32),
               pltpu.VMEM((m, D_MODEL), F32),
               pltpu.VMEM((m, D_RNN), BF16),
               pltpu.VMEM((m, D_GMLP), BF16),
               pltpu.VMEM((m, D_GMLP), BF16),
               pltpu.VMEM((m, D_MODEL), BF16),
               pltpu.VMEM((nb, D_RNN), F32),
               pltpu.VMEM((CONV_W - 1, nb, D_RNN), F32)]
    outs = pl.pallas_call(
        functools.partial(_mixer_kernel, nb=nb, tt=tt, emit_vn=emit_vn),
        grid=(bsz // nb, n_t),
        in_specs=[tile] + [_resident(a.shape) for a in small],
        out_specs=out_specs,
        out_shape=out_shape,
        scratch_shapes=scratch,
        compiler_params=pltpu.CompilerParams(
            dimension_semantics=("arbitrary", "arbitrary"), vmem_limit_bytes=VMEM_LIMIT_BYTES),
        name="mixer",
    )(x4, *small)
    x1 = outs[0].reshape(bsz * t_len, D_MODEL)
    vn = outs[3].reshape(bsz, t_len, D_GMLP) if emit_vn else None
    h_last = outs[1].reshape(bsz, D_RNN)
    conv_new = jnp.transpose(outs[2].reshape(CONV_W - 1, bsz, D_RNN), (1, 0, 2))
    return x1, h_last, conv_new, vn


def _ffn_call(x1, p, *, m):
    rows = x1.shape[0]
    tile = pl.BlockSpec((m, D_MODEL), lambda i: (i, 0))
    small = [p["g_pre_ffn"], p["w_ffn_in"], p["w_ffn_out"], p["g_post_ffn"]]
    return pl.pallas_call(
        functools.partial(_ffn_kernel, m=m),
        grid=(rows // m,),
        in_specs=[tile] + [_resident(a.shape) for a in small],
        out_specs=tile,
        out_shape=jax.ShapeDtypeStruct((rows, D_MODEL), F32),
        scratch_shapes=[pltpu.VMEM((m, D_MODEL), BF16),
                        pltpu.VMEM((m, D_FF), BF16),
                        pltpu.VMEM((m, D_MODEL), F32)],
        compiler_params=pltpu.CompilerParams(
            dimension_semantics=("arbitrary",), vmem_limit_bytes=VMEM_LIMIT_BYTES),
        name="ffn",
    )(x1, *small)


def _block_diag_gates(w_a, w_x):
    eye = jnp.eye(HEADS_PER_MXU, dtype=w_a.dtype)

    def bd(w):
        w = w.reshape(N_GATE_GROUPS, HEADS_PER_MXU, RNN_HEAD_DIM, RNN_HEAD_DIM)
        return jnp.einsum("ghij,hk->ghikj", w, eye).reshape(N_GATE_GROUPS, MXU_N, MXU_N)

    return jnp.concatenate([bd(w_a), bd(w_x)], axis=-1)


def _layer_params(l, g_pre_mix, w_in, conv_w, conv_b, w_a, b_a, w_x, b_x, lam, w_br_a, ln_g, ln_b,
                  w_s, b_s, w_br_b, w_out, g_post_mix, g_pre_ffn, w_ffn_in, w_ffn_out, g_post_ffn):
    row = lambda v: v[l].reshape(1, -1)
    return {
        "g_pre_mix": row(g_pre_mix), "w_in": w_in[l].astype(BF16),
        "conv_w": conv_w[l], "conv_b": row(conv_b),
        "w_gate": _block_diag_gates(w_a[l], w_x[l]).astype(BF16),
        "b_gate": jnp.stack([b_a[l].reshape(-1), b_x[l].reshape(-1)]),
        "lam": row(lam), "w_br_a": w_br_a[l].astype(BF16),
        "ln_g": row(ln_g), "ln_b": row(ln_b),
        "w_s": w_s[l], "b_s_t": b_s[l].T,
        "w_br_b": w_br_b[l].astype(BF16), "w_out": w_out[l].astype(BF16),
        "g_post_mix": row(g_post_mix), "g_pre_ffn": row(g_pre_ffn),
        "w_ffn_in": w_ffn_in[l].astype(BF16), "w_ffn_out": w_ffn_out[l].astype(BF16),
        "g_post_ffn": row(g_post_ffn),
    }


def kernel(x_prompt, x_sample, state_rglru_h, state_rglru_conv, g_pre_mix, w_in, conv_w, conv_b, w_a, b_a, w_x, b_x, lam, w_br_a, ln_g, ln_b, w_s, b_s, w_br_b, w_out, g_post_mix, g_pre_ffn, w_ffn_in, w_ffn_out, g_post_ffn):
    depth = w_in.shape[0]
    bp, tp, _ = x_prompt.shape
    bs, ts, _ = x_sample.shape
    xp, xs = x_prompt, x_sample
    hp_l, cp_l, hs_l, cs_l, vs_l = [], [], [], [], []
    for l in range(depth):
        p = _layer_params(l, g_pre_mix, w_in, conv_w, conv_b, w_a, b_a, w_x, b_x, lam, w_br_a, ln_g,
                          ln_b, w_s, b_s, w_br_b, w_out, g_post_mix, g_pre_ffn, w_ffn_in, w_ffn_out,
                          g_post_ffn)
        h0_p = jnp.zeros((bp, D_RNN), F32)
        c0_p = jnp.zeros((CONV_W - 1, bp, D_RNN), F32)
        x1p, hp, cp, _ = _mixer_call(xp, h0_p, c0_p, p, nb=4, tt=MLP_CHUNK, emit_vn=False)
        xp = _ffn_call(x1p, p, m=512).reshape(bp, tp, D_MODEL)
        c0_s = jnp.transpose(state_rglru_conv[l], (1, 0, 2))
        x1s, hs, cs, vs = _mixer_call(xs, state_rglru_h[l], c0_s, p, nb=bs, tt=ts, emit_vn=True)
        xs = _ffn_call(x1s, p, m=bs * ts).reshape(bs, ts, D_MODEL)
        hp_l.append(hp)
        cp_l.append(cp)
        hs_l.append(hs)
        cs_l.append(cs)
        vs_l.append(vs)
    return (xp, xs, jnp.stack(hp_l), jnp.stack(cp_l), jnp.stack(hs_l), jnp.stack(cs_l),
            jnp.stack(vs_l))
```

```python
import functools
import math

import jax
import jax.numpy as jnp
from jax import lax
from jax.experimental import pallas as pl
from jax.experimental.pallas import tpu as pltpu

D_MODEL = 1024
D_RNN = 1024
N_RNN_HEADS = 16
RNN_HEAD_DIM = D_RNN // N_RNN_HEADS
CONV_W = 4
LRU_C = 8.0
D_GMLP = 1024
N_GMLP_GROUPS = 8
GMLP_GROUP_DIM = D_GMLP // N_GMLP_GROUPS
CHUNK = 64
MLP_CHUNK = 128
D_FF = 2816
EPS = 1e-6

LANES = 128
SUBLANES = 8
MXU_N = 256
N_SLABS = D_RNN // LANES
SLABS_PER_CHUNK = MXU_N // LANES
HEADS_PER_MXU = MXU_N // RNN_HEAD_DIM
N_GATE_GROUPS = D_RNN // MXU_N
ROW_BLOCK = 32
VMEM_LIMIT_BYTES = 56 * 1024 * 1024

BF16 = jnp.bfloat16
F32 = jnp.float32


def _dot(a, b):
    return jnp.dot(a, b, preferred_element_type=F32)


def _rms_scale(v):
    return lax.rsqrt(jnp.mean(v * v, axis=-1, keepdims=True) + EPS)


def _mixer_kernel(x_ref, h0_ref, c0_ref, gpre_ref, win_ref, cw_ref, cb_ref, wg_ref, bg_ref, lam_ref,
                  wbra_ref, lng_ref, lnb_ref, ws_ref, bst_ref, wbrb_ref, wout_ref, gpost_ref,
                  *rest, nb, tt, emit_vn):
    if emit_vn:
        x1_ref, hl_ref, cn_ref, vn_ref = rest[:4]
        scratch = rest[4:]
    else:
        x1_ref, hl_ref, cn_ref = rest[:3]
        vn_ref = None
        scratch = rest[3:]
    (xn_scr, buf1, buf2, wide_scr, gl_scr, gu_scr, ya_scr, yb_scr, vnb_scr, m_scr,
     hcar, ccar) = scratch

    pitch = tt + SUBLANES
    bi = pl.program_id(0)
    ti = pl.program_id(1)
    row_blocks = [(b, r0) for b in range(nb) for r0 in range(0, tt, ROW_BLOCK)]

    def seq_rows(t):
        return pl.ds(t, nb, stride=pitch)

    def lane_tile(s):
        return slice(s * LANES, (s + 1) * LANES)

    @pl.when(ti == 0)
    def _():
        hcar[...] = h0_ref[bi]
        for k in range(CONV_W - 1):
            ccar[k] = c0_ref[k, bi]

    for s in range(N_SLABS):
        for b in range(nb):
            buf1[s, b * pitch + tt:(b + 1) * pitch, :] = jnp.zeros((SUBLANES, LANES), F32)

    gpre = gpre_ref[...]
    for b, r0 in row_blocks:
        xb = x_ref[b, r0:r0 + ROW_BLOCK, :]
        xn_scr[b * tt + r0:b * tt + r0 + ROW_BLOCK, :] = (xb * _rms_scale(xb) * gpre).astype(BF16)

    def in_proj(col0, c):
        lo = col0 + c * MXU_N
        return _dot(xn_scr[...], win_ref[:, lo:lo + MXU_N])

    for c in range(D_RNN // MXU_N):
        xa = in_proj(0, c)
        for b in range(nb):
            for s2 in range(SLABS_PER_CHUNK):
                buf1[SLABS_PER_CHUNK * c + s2, b * pitch:b * pitch + tt, :] = (
                    xa[b * tt:(b + 1) * tt, lane_tile(s2)])

    for s in range(N_SLABS):
        ls = lane_tile(s)
        w = [jnp.broadcast_to(cw_ref[k:k + 1, ls], (nb, LANES)) for k in range(CONV_W)]
        bias = jnp.broadcast_to(cb_ref[:, ls], (nb, LANES))

        def conv_body(t, prev, s=s, w=w, bias=bias):
            p3, p2, p1 = prev
            cur = buf1[s, seq_rows(t), :]
            buf1[s, seq_rows(t), :] = bias + w[0] * p3 + w[1] * p2 + w[2] * p1 + w[3] * cur
            return (p2, p1, cur)

        prev = lax.fori_loop(0, tt, conv_body, tuple(ccar[k, :, ls] for k in range(CONV_W - 1)),
                             unroll=8)
        for k in range(CONV_W - 1):
            ccar[k, :, ls] = prev[k]
    for k in range(CONV_W - 1):
        cn_ref[k, bi] = ccar[k]

    neg_lam = -lam_ref[...]
    softplus_neg_lam = jnp.maximum(neg_lam, 0.0) + jnp.log1p(jnp.exp(-jnp.abs(neg_lam)))
    log2_a_per_r = (-LRU_C * math.log2(math.e)) * softplus_neg_lam
    for j in range(N_GATE_GROUPS):
        s_lo = SLABS_PER_CHUNK * j
        cols = slice(j * MXU_N, (j + 1) * MXU_N)
        xc = jnp.concatenate([buf1[s_lo + s2] for s2 in range(SLABS_PER_CHUNK)], axis=1)
        pre = _dot(xc.astype(BF16), wg_ref[j])
        r = jax.nn.sigmoid(pre[:, :MXU_N] + bg_ref[0:1, cols])
        i_gate = jax.nn.sigmoid(pre[:, MXU_N:] + bg_ref[1:2, cols])
        a = jnp.exp2(r * log2_a_per_r[:, cols])
        mult = jnp.sqrt(1.0 - a * a)
        u = mult * (i_gate * xc)
        for s2 in range(SLABS_PER_CHUNK):
            buf1[s_lo + s2] = a[:, lane_tile(s2)]
            buf2[s_lo + s2] = u[:, lane_tile(s2)]
        gl_scr[:, cols] = jax.nn.gelu(in_proj(D_RNN, j))
        gu_scr[:, cols] = jax.nn.gelu(in_proj(2 * D_RNN, j))
        wide_scr[:, cols] = jax.nn.gelu(in_proj(2 * D_RNN + D_GMLP, j))

    def scan_body(t, hs):
        out = []
        for s in range(N_SLABS):
            h = buf1[s, seq_rows(t), :] * hs[s] + buf2[s, seq_rows(t), :]
            buf2[s, seq_rows(t), :] = h
            out.append(h)
        return tuple(out)

    hs = lax.fori_loop(0, tt, scan_body, tuple(hcar[:, lane_tile(s)] for s in range(N_SLABS)),
                       unroll=8)
    for s in range(N_SLABS):
        hcar[:, lane_tile(s)] = hs[s]
    hl_ref[bi] = hcar[...]

    for s in range(N_SLABS):
        for b in range(nb):
            ya_scr[b * tt:(b + 1) * tt, lane_tile(s)] = (
                buf2[s, b * pitch:b * pitch + tt, :]
                * gl_scr[b * tt:(b + 1) * tt, lane_tile(s)]).astype(BF16)

    lng = lng_ref[...]
    lnb = lnb_ref[...]
    for b, r0 in row_blocks:
        rows = slice(b * tt + r0, b * tt + r0 + ROW_BLOCK)
        gv = wide_scr[rows, :]
        d = gv - jnp.mean(gv, axis=-1, keepdims=True)
        vn = d * lax.rsqrt(jnp.mean(d * d, axis=-1, keepdims=True) + EPS) * lng + lnb
        vnb_scr[rows, :] = vn.astype(BF16)
        if emit_vn:
            vn_ref[b, r0:r0 + ROW_BLOCK, :] = vn

    pos = lax.broadcasted_iota(jnp.int32, (tt, tt), 0) // CHUNK
    qos = lax.broadcasted_iota(jnp.int32, (tt, tt), 1) // CHUNK
    causal = qos <= pos
    for g in range(N_GMLP_GROUPS):
        gcols = lane_tile(g)
        wm = jnp.where(causal, ws_ref[g], 0.0).astype(BF16)
        v_g = jnp.concatenate([vnb_scr[b * tt:(b + 1) * tt, gcols] for b in range(nb)], axis=1)
        sg = _dot(wm, v_g) + bst_ref[:, g:g + 1]
        for b in range(nb):
            yb_scr[b * tt:(b + 1) * tt, gcols] = (
                gu_scr[b * tt:(b + 1) * tt, gcols] * sg[:, lane_tile(b)]).astype(BF16)

    for c in range(D_MODEL // MXU_N):
        cols = slice(c * MXU_N, (c + 1) * MXU_N)
        g_a = jax.nn.sigmoid(in_proj(2 * D_RNN + 2 * D_GMLP, c))
        g_b = jax.nn.sigmoid(in_proj(2 * D_RNN + 2 * D_GMLP + D_MODEL, c))
        o_a = _dot(ya_scr[...], wbra_ref[:, cols])
        o_b = _dot(yb_scr[...], wbrb_ref[:, cols])
        m_scr[:, cols] = (g_a * o_a + g_b * o_b).astype(BF16)

    for c in range(D_MODEL // MXU_N):
        cols = slice(c * MXU_N, (c + 1) * MXU_N)
        wide_scr[:, cols] = _dot(m_scr[...], wout_ref[:, cols])
    gpost = gpost_ref[...]
    for b, r0 in row_blocks:
        mix = wide_scr[b * tt + r0:b * tt + r0 + ROW_BLOCK, :]
        x1_ref[b, r0:r0 + ROW_BLOCK, :] = (
            x_ref[b, r0:r0 + ROW_BLOCK, :] + mix * _rms_scale(mix) * gpost)


def _ffn_kernel(x_ref, gpre_ref, w1_ref, w2_ref, gpost_ref, o_ref, hn_scr, f_scr, wide_scr, *, m):
    gpre = gpre_ref[...]
    for r0 in range(0, m, ROW_BLOCK):
        xb = x_ref[r0:r0 + ROW_BLOCK, :]
        hn_scr[r0:r0 + ROW_BLOCK, :] = (xb * _rms_scale(xb) * gpre).astype(BF16)

    for c in range(D_FF // MXU_N):
        gate = _dot(hn_scr[...], w1_ref[:, c * MXU_N:(c + 1) * MXU_N])
        up = _dot(hn_scr[...], w1_ref[:, D_FF + c * MXU_N:D_FF + (c + 1) * MXU_N])
        f_scr[:, c * MXU_N:(c + 1) * MXU_N] = (jax.nn.silu(gate) * up).astype(BF16)

    for c in range(D_MODEL // MXU_N):
        cols = slice(c * MXU_N, (c + 1) * MXU_N)
        wide_scr[:, cols] = _dot(f_scr[...], w2_ref[:, cols])
    gpost = gpost_ref[...]
    for r0 in range(0, m, ROW_BLOCK):
        f = wide_scr[r0:r0 + ROW_BLOCK, :]
        o_ref[r0:r0 + ROW_BLOCK, :] = x_ref[r0:r0 + ROW_BLOCK, :] + f * _rms_scale(f) * gpost


def _resident(shape):
    nd = len(shape)
    return pl.BlockSpec(shape, lambda *_: (0,) * nd, pipeline_mode=pl.Buffered(1))


def _mixer_call(x, h0, c0, p, *, nb, tt, emit_vn):
    bsz, t_len, _ = x.shape
    n_t = t_len // tt
    m = nb * tt
    pitch = tt + SUBLANES
    n_b = bsz // nb
    x4 = x.reshape(bsz, n_t, tt, D_MODEL)
    tile = pl.BlockSpec((nb, None, tt, D_MODEL), lambda bi, ti: (bi, ti, 0, 0))
    h0 = h0.reshape(n_b, nb, D_RNN)
    c0 = c0.reshape(CONV_W - 1, n_b, nb, D_RNN)
    small = [h0, c0, p["g_pre_mix"], p["w_in"], p["conv_w"], p["conv_b"], p["w_gate"], p["b_gate"],
             p["lam"], p["w_br_a"], p["ln_g"], p["ln_b"], p["w_s"][:, :tt, :tt], p["b_s_t"][:tt],
             p["w_br_b"], p["w_out"], p["g_post_mix"]]
    out_shape = [jax.ShapeDtypeStruct(x4.shape, F32),
                 jax.ShapeDtypeStruct(h0.shape, F32),
                 jax.ShapeDtypeStruct(c0.shape, F32)]
    out_specs = [tile,
                 pl.BlockSpec(h0.shape, lambda bi, ti: (0, 0, 0)),
                 pl.BlockSpec(c0.shape, lambda bi, ti: (0, 0, 0, 0))]
    if emit_vn:
        out_shape.append(jax.ShapeDtypeStruct(x4.shape, F32))
        out_specs.append(tile)
    scratch = [pltpu.VMEM((m, D_MODEL), BF16),
               pltpu.VMEM((N_SLABS, nb * pitch, LANES), F32),
               pltpu.VMEM((N_SLABS, nb * pitch, LANES), F32),
               pltpu.VMEM((m, D_MODEL), F32),
               pltpu.VMEM((m, D_RNN), F32),
               pltpu.VMEM((m, D_GMLP), F32),
               pltpu.VMEM((m, D_RNN), BF16),
               pltpu.VMEM((m, D_GMLP), BF16),
               pltpu.VMEM((m, D_GMLP), BF16),
               pltpu.VMEM((m, D_MODEL), BF16),
               pltpu.VMEM((nb, D_RNN), F32),
               pltpu.VMEM((CONV_W - 1, nb, D_RNN), F32)]
    outs = pl.pallas_call(
        functools.partial(_mixer_kernel, nb=nb, tt=tt, emit_vn=emit_vn),
        grid=(n_b, n_t),
        in_specs=[tile] + [_resident(a.shape) for a in small],
        out_specs=out_specs,
        out_shape=out_shape,
        scratch_shapes=scratch,
        compiler_params=pltpu.CompilerParams(
            dimension_semantics=("arbitrary", "arbitrary"), vmem_limit_bytes=VMEM_LIMIT_BYTES),
        name="mixer",
    )(x4, *small)
    x1 = outs[0].reshape(bsz * t_len, D_MODEL)
    vn = outs[3].reshape(bsz, t_len, D_GMLP) if emit_vn else None
    h_last = outs[1].reshape(bsz, D_RNN)
    conv_new = jnp.transpose(outs[2].reshape(CONV_W - 1, bsz, D_RNN), (1, 0, 2))
    return x1, h_last, conv_new, vn


def _ffn_call(x1, p, *, m):
    rows = x1.shape[0]
    tile = pl.BlockSpec((m, D_MODEL), lambda i: (i, 0))
    small = [p["g_pre_ffn"], p["w_ffn_in"], p["w_ffn_out"], p["g_post_ffn"]]
    return pl.pallas_call(
        functools.partial(_ffn_kernel, m=m),
        grid=(rows // m,),
        in_specs=[tile] + [_resident(a.shape) for a in small],
        out_specs=tile,
        out_shape=jax.ShapeDtypeStruct((rows, D_MODEL), F32),
        scratch_shapes=[pltpu.VMEM((m, D_MODEL), BF16),
                        pltpu.VMEM((m, D_FF), BF16),
                        pltpu.VMEM((m, D_MODEL), F32)],
        compiler_params=pltpu.CompilerParams(
            dimension_semantics=("arbitrary",), vmem_limit_bytes=VMEM_LIMIT_BYTES),
        name="ffn",
    )(x1, *small)


def _block_diag_gates(w_a, w_x):
    eye = jnp.eye(HEADS_PER_MXU, dtype=w_a.dtype)

    def bd(w):
        w = w.reshape(N_GATE_GROUPS, HEADS_PER_MXU, RNN_HEAD_DIM, RNN_HEAD_DIM)
        return jnp.einsum("ghij,hk->ghikj", w, eye).reshape(N_GATE_GROUPS, MXU_N, MXU_N)

    return jnp.concatenate([bd(w_a), bd(w_x)], axis=-1)


def _layer_params(l, g_pre_mix, w_in, conv_w, conv_b, w_a, b_a, w_x, b_x, lam, w_br_a, ln_g, ln_b,
                  w_s, b_s, w_br_b, w_out, g_post_mix, g_pre_ffn, w_ffn_in, w_ffn_out, g_post_ffn):
    row = lambda v: v[l].reshape(1, -1)
    return {
        "g_pre_mix": row(g_pre_mix), "w_in": w_in[l].astype(BF16),
        "conv_w": conv_w[l], "conv_b": row(conv_b),
        "w_gate": _block_diag_gates(w_a[l], w_x[l]).astype(BF16),
        "b_gate": jnp.stack([b_a[l].reshape(-1), b_x[l].reshape(-1)]),
        "lam": row(lam), "w_br_a": w_br_a[l].astype(BF16),
        "ln_g": row(ln_g), "ln_b": row(ln_b),
        "w_s": w_s[l], "b_s_t": b_s[l].T,
        "w_br_b": w_br_b[l].astype(BF16), "w_out": w_out[l].astype(BF16),
        "g_post_mix": row(g_post_mix), "g_pre_ffn": row(g_pre_ffn),
        "w_ffn_in": w_ffn_in[l].astype(BF16), "w_ffn_out": w_ffn_out[l].astype(BF16),
        "g_post_ffn": row(g_post_ffn),
    }


def kernel(x_prompt, x_sample, state_rglru_h, state_rglru_conv, g_pre_mix, w_in, conv_w, conv_b, w_a, b_a, w_x, b_x, lam, w_br_a, ln_g, ln_b, w_s, b_s, w_br_b, w_out, g_post_mix, g_pre_ffn, w_ffn_in, w_ffn_out, g_post_ffn):
    depth = w_in.shape[0]
    bp, tp, _ = x_prompt.shape
    bs, ts, _ = x_sample.shape
    xp, xs = x_prompt, x_sample
    hp_l, cp_l, hs_l, cs_l, vs_l = [], [], [], [], []
    for l in range(depth):
        p = _layer_params(l, g_pre_mix, w_in, conv_w, conv_b, w_a, b_a, w_x, b_x, lam, w_br_a, ln_g,
                          ln_b, w_s, b_s, w_br_b, w_out, g_post_mix, g_pre_ffn, w_ffn_in, w_ffn_out,
                          g_post_ffn)
        h0_p = jnp.zeros((bp, D_RNN), F32)
        c0_p = jnp.zeros((CONV_W - 1, bp, D_RNN), F32)
        x1p, hp, cp, _ = _mixer_call(xp, h0_p, c0_p, p, nb=4, tt=MLP_CHUNK, emit_vn=False)
        xp = _ffn_call(x1p, p, m=512).reshape(bp, tp, D_MODEL)
        c0_s = jnp.transpose(state_rglru_conv[l], (1, 0, 2))
        x1s, hs, cs, vs = _mixer_call(xs, state_rglru_h[l], c0_s, p, nb=bs, tt=ts, emit_vn=True)
        xs = _ffn_call(x1s, p, m=bs * ts).reshape(bs, ts, D_MODEL)
        hp_l.append(hp)
        cp_l.append(cp)
        hs_l.append(hs)
        cs_l.append(cs)
        vs_l.append(vs)
    return (xp, xs, jnp.stack(hp_l), jnp.stack(cp_l), jnp.stack(hs_l), jnp.stack(cs_l),
            jnp.stack(vs_l))
```

```python
import functools
import math

import jax
import jax.numpy as jnp
from jax import lax
from jax.experimental import pallas as pl
from jax.experimental.pallas import tpu as pltpu

D_MODEL = 1024
D_RNN = 1024
N_RNN_HEADS = 16
RNN_HEAD_DIM = D_RNN // N_RNN_HEADS
CONV_W = 4
LRU_C = 8.0
D_GMLP = 1024
N_GMLP_GROUPS = 8
GMLP_GROUP_DIM = D_GMLP // N_GMLP_GROUPS
CHUNK = 64
MLP_CHUNK = 128
D_FF = 2816
EPS = 1e-6
SQRT_GUARD = 1e-30

LANES = 128
SUBLANES = 8
MXU_N = 256
N_SLABS = D_RNN // LANES
SLABS_PER_CHUNK = MXU_N // LANES
HEADS_PER_MXU = MXU_N // RNN_HEAD_DIM
N_GATE_GROUPS = D_RNN // MXU_N
ROW_BLOCK = 32
VMEM_LIMIT_BYTES = 56 * 1024 * 1024

BF16 = jnp.bfloat16
F32 = jnp.float32


def _dot(a, b):
    return jnp.dot(a, b, preferred_element_type=F32)


_GELU_K1 = -2.0 * math.sqrt(2.0 / math.pi) * math.log2(math.e)
_GELU_K2 = 0.044715 * _GELU_K1


def _gelu(x):
    return x / (1.0 + jnp.exp2(x * (_GELU_K1 + _GELU_K2 * (x * x))))


def _rms_scale(v):
    return lax.rsqrt(jnp.mean(v * v, axis=-1, keepdims=True) + EPS)


def _mixer_kernel(x_ref, h0_ref, c0_ref, gpre_ref, win_ref, cw_ref, cb_ref, wg_ref, bg_ref, lam_ref,
                  wbra_ref, lng_ref, lnb_ref, ws_ref, bst_ref, wbrb_ref, wout_ref, gpost_ref,
                  *rest, nb, tt, emit_vn):
    if emit_vn:
        x1_ref, hl_ref, cn_ref, vn_ref = rest[:4]
        scratch = rest[4:]
    else:
        x1_ref, hl_ref, cn_ref = rest[:3]
        vn_ref = None
        scratch = rest[3:]
    (xn_scr, buf1, buf2, wide_scr, gl_scr, gu_scr, ga_scr, gb_scr, ya_scr, yb_scr, vnb_scr, m_scr,
     hcar, xa_buf) = scratch

    pitch = tt + SUBLANES
    bi = pl.program_id(0)
    ti = pl.program_id(1)
    row_blocks = [(b, r0) for b in range(nb) for r0 in range(0, tt, ROW_BLOCK)]

    def seq_rows(t):
        return pl.ds(t, nb, stride=pitch)

    def lane_tile(s):
        return slice(s * LANES, (s + 1) * LANES)

    @pl.when(ti == 0)
    def _():
        hcar[...] = h0_ref[bi]
        for s in range(N_SLABS):
            for b in range(nb):
                xa_buf[s, b * pitch:b * pitch + SUBLANES, :] = c0_ref[bi, b, :, lane_tile(s)]

    for s in range(N_SLABS):
        for b in range(nb):
            buf1[s, b * pitch + tt:(b + 1) * pitch, :] = jnp.zeros((SUBLANES, LANES), F32)

    gpre = gpre_ref[...]
    for b, r0 in row_blocks:
        xb = x_ref[b, r0:r0 + ROW_BLOCK, :]
        xn_scr[b * tt + r0:b * tt + r0 + ROW_BLOCK, :] = (xb * _rms_scale(xb) * gpre).astype(BF16)

    def in_proj(col0, c):
        lo = col0 + c * MXU_N
        return _dot(xn_scr[...], win_ref[:, lo:lo + MXU_N])

    for c in range(D_RNN // MXU_N):
        xa = in_proj(0, c)
        for s2 in range(SLABS_PER_CHUNK):
            s = SLABS_PER_CHUNK * c + s2
            ls = lane_tile(s)
            w = [cw_ref[k:k + 1, ls] for k in range(CONV_W)]
            bias = cb_ref[:, ls]
            for b in range(nb):
                base = b * pitch
                xa_bs = xa[b * tt:(b + 1) * tt, lane_tile(s2)]
                xa_buf[s, base + SUBLANES:base + pitch, :] = xa_bs
                xc = bias + w[CONV_W - 1] * xa_bs
                for k in range(CONV_W - 1):
                    lo = base + SUBLANES - (CONV_W - 1) + k
                    xc = xc + w[k] * xa_buf[s, lo:lo + tt, :]
                buf1[s, base:base + tt, :] = xc
                xa_buf[s, base:base + SUBLANES, :] = xa_bs[tt - SUBLANES:]
                cn_ref[bi, b, :, ls] = xa_bs[tt - SUBLANES:]

    neg_lam = -lam_ref[...]
    softplus_neg_lam = jnp.maximum(neg_lam, 0.0) + jnp.log1p(jnp.exp(-jnp.abs(neg_lam)))
    log2_a_per_r = (-LRU_C * math.log2(math.e)) * softplus_neg_lam
    for j in range(N_GATE_GROUPS):
        s_lo = SLABS_PER_CHUNK * j
        cols = slice(j * MXU_N, (j + 1) * MXU_N)
        xc = jnp.concatenate([buf1[s_lo + s2] for s2 in range(SLABS_PER_CHUNK)], axis=1)
        pre = _dot(xc.astype(BF16), wg_ref[j])
        r = jax.nn.sigmoid(pre[:, :MXU_N] + bg_ref[0:1, cols])
        i_gate = jax.nn.sigmoid(pre[:, MXU_N:] + bg_ref[1:2, cols])
        a = jnp.exp2(r * log2_a_per_r[:, cols])
        y = 1.0 - a * a
        mult = y * lax.rsqrt(jnp.maximum(y, SQRT_GUARD))
        u = mult * (i_gate * xc)
        for s2 in range(SLABS_PER_CHUNK):
            buf1[s_lo + s2] = a[:, lane_tile(s2)]
            buf2[s_lo + s2] = u[:, lane_tile(s2)]
        gl_scr[:, cols] = _gelu(in_proj(D_RNN, j))
        gu_scr[:, cols] = _gelu(in_proj(2 * D_RNN, j))
        wide_scr[:, cols] = _gelu(in_proj(2 * D_RNN + D_GMLP, j))
        ga_scr[:, cols] = jax.nn.sigmoid(in_proj(2 * D_RNN + 2 * D_GMLP, j))
        gb_scr[:, cols] = jax.nn.sigmoid(in_proj(2 * D_RNN + 2 * D_GMLP + D_MODEL, j))

    hs = [hcar[:, lane_tile(s)] for s in range(N_SLABS)]
    for t in range(tt):
        for s in range(N_SLABS):
            hs[s] = buf1[s, seq_rows(t), :] * hs[s] + buf2[s, seq_rows(t), :]
            buf2[s, seq_rows(t), :] = hs[s]
    for s in range(N_SLABS):
        hcar[:, lane_tile(s)] = hs[s]
    hl_ref[bi] = hcar[...]

    for s in range(N_SLABS):
        for b in range(nb):
            ya_scr[b * tt:(b + 1) * tt, lane_tile(s)] = (
                buf2[s, b * pitch:b * pitch + tt, :]
                * gl_scr[b * tt:(b + 1) * tt, lane_tile(s)]).astype(BF16)

    lng = lng_ref[...]
    lnb = lnb_ref[...]
    for b, r0 in row_blocks:
        rows = slice(b * tt + r0, b * tt + r0 + ROW_BLOCK)
        gv = wide_scr[rows, :]
        d = gv - jnp.mean(gv, axis=-1, keepdims=True)
        vn = d * lax.rsqrt(jnp.mean(d * d, axis=-1, keepdims=True) + EPS) * lng + lnb
        vnb_scr[rows, :] = vn.astype(BF16)
        if emit_vn:
            vn_ref[b, r0:r0 + ROW_BLOCK, :] = vn

    pos = lax.broadcasted_iota(jnp.int32, (tt, tt), 0) // CHUNK
    qos = lax.broadcasted_iota(jnp.int32, (tt, tt), 1) // CHUNK
    causal = qos <= pos
    for g in range(N_GMLP_GROUPS):
        gcols = lane_tile(g)
        wm = jnp.where(causal, ws_ref[g], 0.0).astype(BF16)
        v_g = jnp.concatenate([vnb_scr[b * tt:(b + 1) * tt, gcols] for b in range(nb)], axis=1)
        sg = _dot(wm, v_g) + bst_ref[:, g:g + 1]
        for b in range(nb):
            yb_scr[b * tt:(b + 1) * tt, gcols] = (
                gu_scr[b * tt:(b + 1) * tt, gcols] * sg[:, lane_tile(b)]).astype(BF16)

    for c in range(D_MODEL // MXU_N):
        cols = slice(c * MXU_N, (c + 1) * MXU_N)
        o_a = _dot(ya_scr[...], wbra_ref[:, cols])
        o_b = _dot(yb_scr[...], wbrb_ref[:, cols])
        m_scr[:, cols] = (ga_scr[:, cols] * o_a + gb_scr[:, cols] * o_b).astype(BF16)

    for c in range(D_MODEL // MXU_N):
        cols = slice(c * MXU_N, (c + 1) * MXU_N)
        wide_scr[:, cols] = _dot(m_scr[...], wout_ref[:, cols])
    gpost = gpost_ref[...]
    for b, r0 in row_blocks:
        mix = wide_scr[b * tt + r0:b * tt + r0 + ROW_BLOCK, :]
        x1_ref[b, r0:r0 + ROW_BLOCK, :] = (
            x_ref[b, r0:r0 + ROW_BLOCK, :] + mix * _rms_scale(mix) * gpost)


def _ffn_kernel(x_ref, gpre_ref, w1_ref, w2_ref, gpost_ref, o_ref, hn_scr, f_scr, wide_scr, *, m):
    gpre = gpre_ref[...]
    for r0 in range(0, m, ROW_BLOCK):
        xb = x_ref[r0:r0 + ROW_BLOCK, :]
        hn_scr[r0:r0 + ROW_BLOCK, :] = (xb * _rms_scale(xb) * gpre).astype(BF16)

    for c in range(D_FF // MXU_N):
        gate = _dot(hn_scr[...], w1_ref[:, c * MXU_N:(c + 1) * MXU_N])
        up = _dot(hn_scr[...], w1_ref[:, D_FF + c * MXU_N:D_FF + (c + 1) * MXU_N])
        f_scr[:, c * MXU_N:(c + 1) * MXU_N] = (jax.nn.silu(gate) * up).astype(BF16)

    for c in range(D_MODEL // MXU_N):
        cols = slice(c * MXU_N, (c + 1) * MXU_N)
        wide_scr[:, cols] = _dot(f_scr[...], w2_ref[:, cols])
    gpost = gpost_ref[...]
    for r0 in range(0, m, ROW_BLOCK):
        f = wide_scr[r0:r0 + ROW_BLOCK, :]
        o_ref[r0:r0 + ROW_BLOCK, :] = x_ref[r0:r0 + ROW_BLOCK, :] + f * _rms_scale(f) * gpost


def _resident(shape):
    nd = len(shape)
    return pl.BlockSpec(shape, lambda *_: (0,) * nd, pipeline_mode=pl.Buffered(1))


def _mixer_call(x, h0, c0, p, *, nb, tt, emit_vn):
    bsz, t_len, _ = x.shape
    n_t = t_len // tt
    m = nb * tt
    pitch = tt + SUBLANES
    n_b = bsz // nb
    x4 = x.reshape(bsz, n_t, tt, D_MODEL)
    tile = pl.BlockSpec((nb, None, tt, D_MODEL), lambda bi, ti: (bi, ti, 0, 0))
    h0 = h0.reshape(n_b, nb, D_RNN)
    c0 = jnp.pad(c0, ((0, 0), (SUBLANES - (CONV_W - 1), 0), (0, 0))).reshape(n_b, nb, SUBLANES, D_RNN)
    small = [h0, c0, p["g_pre_mix"], p["w_in"], p["conv_w"], p["conv_b"], p["w_gate"], p["b_gate"],
             p["lam"], p["w_br_a"], p["ln_g"], p["ln_b"], p["w_s"][:, :tt, :tt], p["b_s_t"][:tt],
             p["w_br_b"], p["w_out"], p["g_post_mix"]]
    out_shape = [jax.ShapeDtypeStruct(x4.shape, F32),
                 jax.ShapeDtypeStruct(h0.shape, F32),
                 jax.ShapeDtypeStruct(c0.shape, F32)]
    out_specs = [tile,
                 pl.BlockSpec(h0.shape, lambda bi, ti: (0, 0, 0)),
                 pl.BlockSpec(c0.shape, lambda bi, ti: (0, 0, 0, 0))]
    if emit_vn:
        out_shape.append(jax.ShapeDtypeStruct(x4.shape, F32))
        out_specs.append(tile)
    scratch = [pltpu.VMEM((m, D_MODEL), BF16),
               pltpu.VMEM((N_SLABS, nb * pitch, LANES), F32),
               pltpu.VMEM((N_SLABS, nb * pitch, LANES), F32),
               pltpu.VMEM((m, D_MODEL), F32),
               pltpu.VMEM((m, D_RNN), F32),
               pltpu.VMEM((m, D_GMLP), F32),
               pltpu.VMEM((m, D_MODEL), F32),
               pltpu.VMEM((m, D_MODEL), F32),
               pltpu.VMEM((m, D_RNN), BF16),
               pltpu.VMEM((m, D_GMLP), BF16),
               pltpu.VMEM((m, D_GMLP), BF16),
               pltpu.VMEM((m, D_MODEL), BF16),
               pltpu.VMEM((nb, D_RNN), F32),
               pltpu.VMEM((N_SLABS, nb * pitch, LANES), F32)]
    outs = pl.pallas_call(
        functools.partial(_mixer_kernel, nb=nb, tt=tt, emit_vn=emit_vn),
        grid=(n_b, n_t),
        in_specs=[tile] + [_resident(a.shape) for a in small],
        out_specs=out_specs,
        out_shape=out_shape,
        scratch_shapes=scratch,
        compiler_params=pltpu.CompilerParams(
            dimension_semantics=("arbitrary", "arbitrary"), vmem_limit_bytes=VMEM_LIMIT_BYTES),
        name="mixer",
    )(x4, *small)
    x1 = outs[0].reshape(bsz * t_len, D_MODEL)
    vn = outs[3].reshape(bsz, t_len, D_GMLP) if emit_vn else None
    h_last = outs[1].reshape(bsz, D_RNN)
    conv_new = outs[2].reshape(bsz, SUBLANES, D_RNN)[:, SUBLANES - (CONV_W - 1):]
    return x1, h_last, conv_new, vn


def _ffn_call(x1, p, *, m):
    rows = x1.shape[0]
    tile = pl.BlockSpec((m, D_MODEL), lambda i: (i, 0))
    small = [p["g_pre_ffn"], p["w_ffn_in"], p["w_ffn_out"], p["g_post_ffn"]]
    return pl.pallas_call(
        functools.partial(_ffn_kernel, m=m),
        grid=(rows // m,),
        in_specs=[tile] + [_resident(a.shape) for a in small],
        out_specs=tile,
        out_shape=jax.ShapeDtypeStruct((rows, D_MODEL), F32),
        scratch_shapes=[pltpu.VMEM((m, D_MODEL), BF16),
                        pltpu.VMEM((m, D_FF), BF16),
                        pltpu.VMEM((m, D_MODEL), F32)],
        compiler_params=pltpu.CompilerParams(
            dimension_semantics=("arbitrary",), vmem_limit_bytes=VMEM_LIMIT_BYTES),
        name="ffn",
    )(x1, *small)


def _block_diag_gates(w_a, w_x):
    eye = jnp.eye(HEADS_PER_MXU, dtype=w_a.dtype)

    def bd(w):
        w = w.reshape(N_GATE_GROUPS, HEADS_PER_MXU, RNN_HEAD_DIM, RNN_HEAD_DIM)
        return jnp.einsum("ghij,hk->ghikj", w, eye).reshape(N_GATE_GROUPS, MXU_N, MXU_N)

    return jnp.concatenate([bd(w_a), bd(w_x)], axis=-1)


def _layer_params(l, g_pre_mix, w_in, conv_w, conv_b, w_a, b_a, w_x, b_x, lam, w_br_a, ln_g, ln_b,
                  w_s, b_s, w_br_b, w_out, g_post_mix, g_pre_ffn, w_ffn_in, w_ffn_out, g_post_ffn):
    row = lambda v: v[l].reshape(1, -1)
    return {
        "g_pre_mix": row(g_pre_mix), "w_in": w_in[l].astype(BF16),
        "conv_w": conv_w[l], "conv_b": row(conv_b),
        "w_gate": _block_diag_gates(w_a[l], w_x[l]).astype(BF16),
        "b_gate": jnp.stack([b_a[l].reshape(-1), b_x[l].reshape(-1)]),
        "lam": row(lam), "w_br_a": w_br_a[l].astype(BF16),
        "ln_g": row(ln_g), "ln_b": row(ln_b),
        "w_s": w_s[l], "b_s_t": b_s[l].T,
        "w_br_b": w_br_b[l].astype(BF16), "w_out": w_out[l].astype(BF16),
        "g_post_mix": row(g_post_mix), "g_pre_ffn": row(g_pre_ffn),
        "w_ffn_in": w_ffn_in[l].astype(BF16), "w_ffn_out": w_ffn_out[l].astype(BF16),
        "g_post_ffn": row(g_post_ffn),
    }


def kernel(x_prompt, x_sample, state_rglru_h, state_rglru_conv, g_pre_mix, w_in, conv_w, conv_b, w_a, b_a, w_x, b_x, lam, w_br_a, ln_g, ln_b, w_s, b_s, w_br_b, w_out, g_post_mix, g_pre_ffn, w_ffn_in, w_ffn_out, g_post_ffn):
    depth = w_in.shape[0]
    bp, tp, _ = x_prompt.shape
    bs, ts, _ = x_sample.shape
    xp, xs = x_prompt, x_sample
    hp_l, cp_l, hs_l, cs_l, vs_l = [], [], [], [], []
    for l in range(depth):
        p = _layer_params(l, g_pre_mix, w_in, conv_w, conv_b, w_a, b_a, w_x, b_x, lam, w_br_a, ln_g,
                          ln_b, w_s, b_s, w_br_b, w_out, g_post_mix, g_pre_ffn, w_ffn_in, w_ffn_out,
                          g_post_ffn)
        h0_p = jnp.zeros((bp, D_RNN), F32)
        c0_p = jnp.zeros((bp, CONV_W - 1, D_RNN), F32)
        x1p, hp, cp, _ = _mixer_call(xp, h0_p, c0_p, p, nb=4, tt=MLP_CHUNK, emit_vn=False)
        xp = _ffn_call(x1p, p, m=512).reshape(bp, tp, D_MODEL)
        x1s, hs, cs, vs = _mixer_call(xs, state_rglru_h[l], state_rglru_conv[l], p, nb=bs, tt=ts,
                                      emit_vn=True)
        xs = _ffn_call(x1s, p, m=bs * ts).reshape(bs, ts, D_MODEL)
        hp_l.append(hp)
        cp_l.append(cp)
        hs_l.append(hs)
        cs_l.append(cs)
        vs_l.append(vs)
    return (xp, xs, jnp.stack(hp_l), jnp.stack(cp_l), jnp.stack(hs_l), jnp.stack(cs_l),
            jnp.stack(vs_l))
```

```python
import functools
import math

import jax
import jax.numpy as jnp
from jax import lax
from jax.experimental import pallas as pl
from jax.experimental.pallas import tpu as pltpu

D_MODEL = 1024
D_RNN = 1024
N_RNN_HEADS = 16
RNN_HEAD_DIM = D_RNN // N_RNN_HEADS
CONV_W = 4
LRU_C = 8.0
D_GMLP = 1024
N_GMLP_GROUPS = 8
GMLP_GROUP_DIM = D_GMLP // N_GMLP_GROUPS
CHUNK = 64
MLP_CHUNK = 128
D_FF = 2816
EPS = 1e-6
SQRT_GUARD = 1e-30

LANES = 128
SUBLANES = 8
MXU_N = 256
N_SLABS = D_RNN // LANES
SLABS_PER_CHUNK = MXU_N // LANES
HEADS_PER_MXU = MXU_N // RNN_HEAD_DIM
N_GATE_GROUPS = D_RNN // MXU_N
ROW_BLOCK = 32
VMEM_LIMIT_BYTES = 56 * 1024 * 1024

BF16 = jnp.bfloat16
F32 = jnp.float32


def _dot(a, b):
    return jnp.dot(a, b, preferred_element_type=F32)


_GELU_K1 = -2.0 * math.sqrt(2.0 / math.pi) * math.log2(math.e)
_GELU_K2 = 0.044715 * _GELU_K1


def _gelu(x):
    return x / (1.0 + jnp.exp2(x * (_GELU_K1 + _GELU_K2 * (x * x))))


def _rms_scale(v):
    return lax.rsqrt(jnp.mean(v * v, axis=-1, keepdims=True) + EPS)


def _mixer_kernel(x_ref, h0_ref, c0_ref, gpre_ref, win_ref, cw_ref, cb_ref, wg_ref, bg_ref, lam_ref,
                  wbra_ref, lng_ref, lnb_ref, ws_ref, bst_ref, wbrb_ref, wout_ref, gpost_ref,
                  *rest, nb, tt, emit_vn):
    if emit_vn:
        x1_ref, hl_ref, cn_ref, vn_ref = rest[:4]
        scratch = rest[4:]
    else:
        x1_ref, hl_ref, cn_ref = rest[:3]
        vn_ref = None
        scratch = rest[3:]
    (xn_scr, buf1, buf2, wide_scr, gl_scr, gu_scr, ga_scr, gb_scr, ya_scr, yb_scr, vnb_scr, m_scr,
     hcar, xa_buf) = scratch

    pitch = tt + SUBLANES
    bi = pl.program_id(0)
    ti = pl.program_id(1)
    row_blocks = [(b, r0) for b in range(nb) for r0 in range(0, tt, ROW_BLOCK)]

    def seq_rows(t):
        return pl.ds(t, nb, stride=pitch)

    def lane_tile(s):
        return slice(s * LANES, (s + 1) * LANES)

    @pl.when(ti == 0)
    def _():
        hcar[...] = h0_ref[bi]
        for s in range(N_SLABS):
            for b in range(nb):
                xa_buf[s, b * pitch:b * pitch + SUBLANES, :] = c0_ref[bi, b, :, lane_tile(s)]

    for s in range(N_SLABS):
        for b in range(nb):
            buf1[s, b * pitch + tt:(b + 1) * pitch, :] = jnp.zeros((SUBLANES, LANES), F32)

    gpre = gpre_ref[...]
    for b, r0 in row_blocks:
        xb = x_ref[b, r0:r0 + ROW_BLOCK, :]
        xn_scr[b * tt + r0:b * tt + r0 + ROW_BLOCK, :] = (xb * _rms_scale(xb) * gpre).astype(BF16)

    def in_proj(col0, c):
        lo = col0 + c * MXU_N
        return _dot(xn_scr[...], win_ref[:, lo:lo + MXU_N])

    for c in range(D_RNN // MXU_N):
        xa = in_proj(0, c)
        for s2 in range(SLABS_PER_CHUNK):
            s = SLABS_PER_CHUNK * c + s2
            ls = lane_tile(s)
            w = [cw_ref[k:k + 1, ls] for k in range(CONV_W)]
            bias = cb_ref[:, ls]
            for b in range(nb):
                base = b * pitch
                xa_bs = xa[b * tt:(b + 1) * tt, lane_tile(s2)]
                xa_buf[s, base + SUBLANES:base + pitch, :] = xa_bs
                xc = bias + w[CONV_W - 1] * xa_bs
                for k in range(CONV_W - 1):
                    lo = base + SUBLANES - (CONV_W - 1) + k
                    xc = xc + w[k] * xa_buf[s, lo:lo + tt, :]
                buf1[s, base:base + tt, :] = xc
                xa_buf[s, base:base + SUBLANES, :] = xa_bs[tt - SUBLANES:]
                cn_ref[bi, b, :, ls] = xa_bs[tt - SUBLANES:]

    neg_lam = -lam_ref[...]
    softplus_neg_lam = jnp.maximum(neg_lam, 0.0) + jnp.log1p(jnp.exp(-jnp.abs(neg_lam)))
    log2_a_per_r = (-LRU_C * math.log2(math.e)) * softplus_neg_lam
    for j in range(N_GATE_GROUPS):
        s_lo = SLABS_PER_CHUNK * j
        cols = slice(j * MXU_N, (j + 1) * MXU_N)
        xc = jnp.concatenate([buf1[s_lo + s2] for s2 in range(SLABS_PER_CHUNK)], axis=1)
        pre = _dot(xc.astype(BF16), wg_ref[j])
        r = jax.nn.sigmoid(pre[:, :MXU_N] + bg_ref[0:1, cols])
        i_gate = jax.nn.sigmoid(pre[:, MXU_N:] + bg_ref[1:2, cols])
        a = jnp.exp2(r * log2_a_per_r[:, cols])
        y = 1.0 - a * a
        mult = y * lax.rsqrt(jnp.maximum(y, SQRT_GUARD))
        u = mult * (i_gate * xc)
        for s2 in range(SLABS_PER_CHUNK):
            buf1[s_lo + s2] = a[:, lane_tile(s2)]
            buf2[s_lo + s2] = u[:, lane_tile(s2)]
        gl_scr[:, cols] = _gelu(in_proj(D_RNN, j))
        gu_scr[:, cols] = _gelu(in_proj(2 * D_RNN, j))
        wide_scr[:, cols] = _gelu(in_proj(2 * D_RNN + D_GMLP, j))
        ga_scr[:, cols] = jax.nn.sigmoid(in_proj(2 * D_RNN + 2 * D_GMLP, j))
        gb_scr[:, cols] = jax.nn.sigmoid(in_proj(2 * D_RNN + 2 * D_GMLP + D_MODEL, j))

    def scan_body(t, hs):
        out = []
        for s in range(N_SLABS):
            h = buf1[s, seq_rows(t), :] * hs[s] + buf2[s, seq_rows(t), :]
            buf2[s, seq_rows(t), :] = h
            out.append(h)
        return tuple(out)

    hs = lax.fori_loop(0, tt, scan_body, tuple(hcar[:, lane_tile(s)] for s in range(N_SLABS)),
                       unroll=8)
    for s in range(N_SLABS):
        hcar[:, lane_tile(s)] = hs[s]
    hl_ref[bi] = hcar[...]

    for s in range(N_SLABS):
        for b in range(nb):
            ya_scr[b * tt:(b + 1) * tt, lane_tile(s)] = (
                buf2[s, b * pitch:b * pitch + tt, :]
                * gl_scr[b * tt:(b + 1) * tt, lane_tile(s)]).astype(BF16)

    lng = lng_ref[...]
    lnb = lnb_ref[...]
    for b, r0 in row_blocks:
        rows = slice(b * tt + r0, b * tt + r0 + ROW_BLOCK)
        gv = wide_scr[rows, :]
        d = gv - jnp.mean(gv, axis=-1, keepdims=True)
        vn = d * lax.rsqrt(jnp.mean(d * d, axis=-1, keepdims=True) + EPS) * lng + lnb
        vnb_scr[rows, :] = vn.astype(BF16)
        if emit_vn:
            vn_ref[b, r0:r0 + ROW_BLOCK, :] = vn

    pos = lax.broadcasted_iota(jnp.int32, (tt, tt), 0) // CHUNK
    qos = lax.broadcasted_iota(jnp.int32, (tt, tt), 1) // CHUNK
    causal = qos <= pos
    for g in range(N_GMLP_GROUPS):
        gcols = lane_tile(g)
        wm = jnp.where(causal, ws_ref[g], 0.0).astype(BF16)
        v_g = jnp.concatenate([vnb_scr[b * tt:(b + 1) * tt, gcols] for b in range(nb)], axis=1)
        sg = _dot(wm, v_g) + bst_ref[:, g:g + 1]
        for b in range(nb):
            yb_scr[b * tt:(b + 1) * tt, gcols] = (
                gu_scr[b * tt:(b + 1) * tt, gcols] * sg[:, lane_tile(b)]).astype(BF16)

    for c in range(D_MODEL // MXU_N):
        cols = slice(c * MXU_N, (c + 1) * MXU_N)
        o_a = _dot(ya_scr[...], wbra_ref[:, cols])
        o_b = _dot(yb_scr[...], wbrb_ref[:, cols])
        m_scr[:, cols] = (ga_scr[:, cols] * o_a + gb_scr[:, cols] * o_b).astype(BF16)

    for c in range(D_MODEL // MXU_N):
        cols = slice(c * MXU_N, (c + 1) * MXU_N)
        wide_scr[:, cols] = _dot(m_scr[...], wout_ref[:, cols])
    gpost = gpost_ref[...]
    for b, r0 in row_blocks:
        mix = wide_scr[b * tt + r0:b * tt + r0 + ROW_BLOCK, :]
        x1_ref[b, r0:r0 + ROW_BLOCK, :] = (
            x_ref[b, r0:r0 + ROW_BLOCK, :] + mix * _rms_scale(mix) * gpost)


def _ffn_kernel(x_ref, gpre_ref, w1_ref, w2_ref, gpost_ref, o_ref, hn_scr, f_scr, wide_scr, *, m):
    gpre = gpre_ref[...]
    for r0 in range(0, m, ROW_BLOCK):
        xb = x_ref[r0:r0 + ROW_BLOCK, :]
        hn_scr[r0:r0 + ROW_BLOCK, :] = (xb * _rms_scale(xb) * gpre).astype(BF16)

    for c in range(D_FF // MXU_N):
        gate = _dot(hn_scr[...], w1_ref[:, c * MXU_N:(c + 1) * MXU_N])
        up = _dot(hn_scr[...], w1_ref[:, D_FF + c * MXU_N:D_FF + (c + 1) * MXU_N])
        f_scr[:, c * MXU_N:(c + 1) * MXU_N] = (jax.nn.silu(gate) * up).astype(BF16)

    for c in range(D_MODEL // MXU_N):
        cols = slice(c * MXU_N, (c + 1) * MXU_N)
        wide_scr[:, cols] = _dot(f_scr[...], w2_ref[:, cols])
    gpost = gpost_ref[...]
    for r0 in range(0, m, ROW_BLOCK):
        f = wide_scr[r0:r0 + ROW_BLOCK, :]
        o_ref[r0:r0 + ROW_BLOCK, :] = x_ref[r0:r0 + ROW_BLOCK, :] + f * _rms_scale(f) * gpost


def _resident(shape):
    nd = len(shape)
    return pl.BlockSpec(shape, lambda *_: (0,) * nd, pipeline_mode=pl.Buffered(1))


def _mixer_call(x, h0, c0, p, *, nb, tt, emit_vn):
    bsz, t_len, _ = x.shape
    n_t = t_len // tt
    m = nb * tt
    pitch = tt + SUBLANES
    n_b = bsz // nb
    x4 = x.reshape(bsz, n_t, tt, D_MODEL)
    tile = pl.BlockSpec((nb, None, tt, D_MODEL), lambda bi, ti: (bi, ti, 0, 0))
    h0 = h0.reshape(n_b, nb, D_RNN)
    c0 = jnp.pad(c0, ((0, 0), (SUBLANES - (CONV_W - 1), 0), (0, 0))).reshape(n_b, nb, SUBLANES, D_RNN)
    small = [h0, c0, p["g_pre_mix"], p["w_in"], p["conv_w"], p["conv_b"], p["w_gate"], p["b_gate"],
             p["lam"], p["w_br_a"], p["ln_g"], p["ln_b"], p["w_s"][:, :tt, :tt], p["b_s_t"][:tt],
             p["w_br_b"], p["w_out"], p["g_post_mix"]]
    out_shape = [jax.ShapeDtypeStruct(x4.shape, F32),
                 jax.ShapeDtypeStruct(h0.shape, F32),
                 jax.ShapeDtypeStruct(c0.shape, F32)]
    out_specs = [tile,
                 pl.BlockSpec(h0.shape, lambda bi, ti: (0, 0, 0)),
                 pl.BlockSpec(c0.shape, lambda bi, ti: (0, 0, 0, 0))]
    if emit_vn:
        out_shape.append(jax.ShapeDtypeStruct(x4.shape, F32))
        out_specs.append(tile)
    scratch = [pltpu.VMEM((m, D_MODEL), BF16),
               pltpu.VMEM((N_SLABS, nb * pitch, LANES), F32),
               pltpu.VMEM((N_SLABS, nb * pitch, LANES), F32),
               pltpu.VMEM((m, D_MODEL), F32),
               pltpu.VMEM((m, D_RNN), F32),
               pltpu.VMEM((m, D_GMLP), F32),
               pltpu.VMEM((m, D_MODEL), F32),
               pltpu.VMEM((m, D_MODEL), F32),
               pltpu.VMEM((m, D_RNN), BF16),
               pltpu.VMEM((m, D_GMLP), BF16),
               pltpu.VMEM((m, D_GMLP), BF16),
               pltpu.VMEM((m, D_MODEL), BF16),
               pltpu.VMEM((nb, D_RNN), F32),
               pltpu.VMEM((N_SLABS, nb * pitch, LANES), F32)]
    outs = pl.pallas_call(
        functools.partial(_mixer_kernel, nb=nb, tt=tt, emit_vn=emit_vn),
        grid=(n_b, n_t),
        in_specs=[tile] + [_resident(a.shape) for a in small],
        out_specs=out_specs,
        out_shape=out_shape,
        scratch_shapes=scratch,
        compiler_params=pltpu.CompilerParams(
            dimension_semantics=("arbitrary", "arbitrary"), vmem_limit_bytes=VMEM_LIMIT_BYTES),
        name="mixer",
    )(x4, *small)
    x1 = outs[0].reshape(bsz * t_len, D_MODEL)
    vn = outs[3].reshape(bsz, t_len, D_GMLP) if emit_vn else None
    h_last = outs[1].reshape(bsz, D_RNN)
    conv_new = outs[2].reshape(bsz, SUBLANES, D_RNN)[:, SUBLANES - (CONV_W - 1):]
    return x1, h_last, conv_new, vn


def _ffn_call(x1, p, *, m):
    rows = x1.shape[0]
    tile = pl.BlockSpec((m, D_MODEL), lambda i: (i, 0))
    small = [p["g_pre_ffn"], p["w_ffn_in"], p["w_ffn_out"], p["g_post_ffn"]]
    return pl.pallas_call(
        functools.partial(_ffn_kernel, m=m),
        grid=(rows // m,),
        in_specs=[tile] + [_resident(a.shape) for a in small],
        out_specs=tile,
        out_shape=jax.ShapeDtypeStruct((rows, D_MODEL), F32),
        scratch_shapes=[pltpu.VMEM((m, D_MODEL), BF16),
                        pltpu.VMEM((m, D_FF), BF16),
                        pltpu.VMEM((m, D_MODEL), F32)],
        compiler_params=pltpu.CompilerParams(
            dimension_semantics=("arbitrary",), vmem_limit_bytes=VMEM_LIMIT_BYTES),
        name="ffn",
    )(x1, *small)


def _block_diag_gates(w_a, w_x):
    eye = jnp.eye(HEADS_PER_MXU, dtype=w_a.dtype)

    def bd(w):
        w = w.reshape(N_GATE_GROUPS, HEADS_PER_MXU, RNN_HEAD_DIM, RNN_HEAD_DIM)
        return jnp.einsum("ghij,hk->ghikj", w, eye).reshape(N_GATE_GROUPS, MXU_N, MXU_N)

    return jnp.concatenate([bd(w_a), bd(w_x)], axis=-1)


def _layer_params(l, g_pre_mix, w_in, conv_w, conv_b, w_a, b_a, w_x, b_x, lam, w_br_a, ln_g, ln_b,
                  w_s, b_s, w_br_b, w_out, g_post_mix, g_pre_ffn, w_ffn_in, w_ffn_out, g_post_ffn):
    row = lambda v: v[l].reshape(1, -1)
    return {
        "g_pre_mix": row(g_pre_mix), "w_in": w_in[l].astype(BF16),
        "conv_w": conv_w[l], "conv_b": row(conv_b),
        "w_gate": _block_diag_gates(w_a[l], w_x[l]).astype(BF16),
        "b_gate": jnp.stack([b_a[l].reshape(-1), b_x[l].reshape(-1)]),
        "lam": row(lam), "w_br_a": w_br_a[l].astype(BF16),
        "ln_g": row(ln_g), "ln_b": row(ln_b),
        "w_s": w_s[l], "b_s_t": b_s[l].T,
        "w_br_b": w_br_b[l].astype(BF16), "w_out": w_out[l].astype(BF16),
        "g_post_mix": row(g_post_mix), "g_pre_ffn": row(g_pre_ffn),
        "w_ffn_in": w_ffn_in[l].astype(BF16), "w_ffn_out": w_ffn_out[l].astype(BF16),
        "g_post_ffn": row(g_post_ffn),
    }


def kernel(x_prompt, x_sample, state_rglru_h, state_rglru_conv, g_pre_mix, w_in, conv_w, conv_b, w_a, b_a, w_x, b_x, lam, w_br_a, ln_g, ln_b, w_s, b_s, w_br_b, w_out, g_post_mix, g_pre_ffn, w_ffn_in, w_ffn_out, g_post_ffn):
    depth = w_in.shape[0]
    bp, tp, _ = x_prompt.shape
    bs, ts, _ = x_sample.shape
    xp, xs = x_prompt, x_sample
    hp_l, cp_l, hs_l, cs_l, vs_l = [], [], [], [], []
    for l in range(depth):
        p = _layer_params(l, g_pre_mix, w_in, conv_w, conv_b, w_a, b_a, w_x, b_x, lam, w_br_a, ln_g,
                          ln_b, w_s, b_s, w_br_b, w_out, g_post_mix, g_pre_ffn, w_ffn_in, w_ffn_out,
                          g_post_ffn)
        h0_p = jnp.zeros((bp, D_RNN), F32)
        c0_p = jnp.zeros((bp, CONV_W - 1, D_RNN), F32)
        x1p, hp, cp, _ = _mixer_call(xp, h0_p, c0_p, p, nb=4, tt=MLP_CHUNK, emit_vn=False)
        xp = _ffn_call(x1p, p, m=512).reshape(bp, tp, D_MODEL)
        x1s, hs, cs, vs = _mixer_call(xs, state_rglru_h[l], state_rglru_conv[l], p, nb=bs, tt=ts,
                                      emit_vn=True)
        xs = _ffn_call(x1s, p, m=bs * ts).reshape(bs, ts, D_MODEL)
        hp_l.append(hp)
        cp_l.append(cp)
        hs_l.append(hs)
        cs_l.append(cs)
        vs_l.append(vs)
    return (xp, xs, jnp.stack(hp_l), jnp.stack(cp_l), jnp.stack(hs_l), jnp.stack(cs_l),
            jnp.stack(vs_l))
```

```python
import functools
import math

import jax
import jax.numpy as jnp
from jax import lax
from jax.experimental import pallas as pl
from jax.experimental.pallas import tpu as pltpu

D_MODEL = 1024
D_RNN = 1024
N_RNN_HEADS = 16
RNN_HEAD_DIM = D_RNN // N_RNN_HEADS
CONV_W = 4
LRU_C = 8.0
D_GMLP = 1024
N_GMLP_GROUPS = 8
GMLP_GROUP_DIM = D_GMLP // N_GMLP_GROUPS
CHUNK = 64
MLP_CHUNK = 128
D_FF = 2816
EPS = 1e-6
SQRT_GUARD = 1e-30

LANES = 128
SUBLANES = 8
MXU_N = 256
N_SLABS = D_RNN // LANES
SLABS_PER_CHUNK = MXU_N // LANES
HEADS_PER_MXU = MXU_N // RNN_HEAD_DIM
N_GATE_GROUPS = D_RNN // MXU_N
ROW_BLOCK = 32
VMEM_LIMIT_BYTES = 56 * 1024 * 1024

BF16 = jnp.bfloat16
F32 = jnp.float32


def _dot(a, b):
    return jnp.dot(a, b, preferred_element_type=F32)


_GELU_K1 = -2.0 * math.sqrt(2.0 / math.pi) * math.log2(math.e)
_GELU_K2 = 0.044715 * _GELU_K1


def _gelu(x):
    return x / (1.0 + jnp.exp2(x * (_GELU_K1 + _GELU_K2 * (x * x))))


def _rms_scale(v):
    return lax.rsqrt(jnp.mean(v * v, axis=-1, keepdims=True) + EPS)


def _mixer_kernel(x_ref, h0_ref, c0_ref, gpre_ref, win_ref, cw_ref, cb_ref, wg_ref, bg_ref, lam_ref,
                  wbra_ref, lng_ref, lnb_ref, ws_ref, bst_ref, wbrb_ref, wout_ref, gpost_ref,
                  *rest, nb, tt, emit_vn):
    if emit_vn:
        x1_ref, hl_ref, cn_ref, vn_ref = rest[:4]
        scratch = rest[4:]
    else:
        x1_ref, hl_ref, cn_ref = rest[:3]
        vn_ref = None
        scratch = rest[3:]
    (xn_scr, buf1, buf2, wide_scr, gl_scr, gu_scr, ga_scr, gb_scr, ya_scr, yb_scr, vnb_scr, m_scr,
     hcar, xa_buf) = scratch

    pitch = tt + SUBLANES
    bi = pl.program_id(0)
    ti = pl.program_id(1)
    row_blocks = [(b, r0) for b in range(nb) for r0 in range(0, tt, ROW_BLOCK)]

    def seq_rows(t):
        return pl.ds(t, nb, stride=pitch)

    def lane_tile(s):
        return slice(s * LANES, (s + 1) * LANES)

    @pl.when(ti == 0)
    def _():
        hcar[...] = h0_ref[bi]
        for s in range(N_SLABS):
            for b in range(nb):
                xa_buf[s, b * pitch:b * pitch + SUBLANES, :] = c0_ref[bi, b, :, lane_tile(s)]

    for s in range(N_SLABS):
        for b in range(nb):
            buf1[s, b * pitch + tt:(b + 1) * pitch, :] = jnp.zeros((SUBLANES, LANES), F32)

    gpre = gpre_ref[...]
    for b, r0 in row_blocks:
        xb = x_ref[b, r0:r0 + ROW_BLOCK, :]
        xn_scr[b * tt + r0:b * tt + r0 + ROW_BLOCK, :] = (xb * _rms_scale(xb) * gpre).astype(BF16)

    def in_proj(col0, c):
        lo = col0 + c * MXU_N
        return _dot(xn_scr[...], win_ref[:, lo:lo + MXU_N])

    for c in range(D_RNN // MXU_N):
        xa = in_proj(0, c)
        for s2 in range(SLABS_PER_CHUNK):
            s = SLABS_PER_CHUNK * c + s2
            ls = lane_tile(s)
            w = [cw_ref[k:k + 1, ls] for k in range(CONV_W)]
            bias = cb_ref[:, ls]
            for b in range(nb):
                base = b * pitch
                xa_bs = xa[b * tt:(b + 1) * tt, lane_tile(s2)]
                xa_buf[s, base + SUBLANES:base + pitch, :] = xa_bs
                xc = bias + w[CONV_W - 1] * xa_bs
                for k in range(CONV_W - 1):
                    lo = base + SUBLANES - (CONV_W - 1) + k
                    xc = xc + w[k] * xa_buf[s, lo:lo + tt, :]
                buf1[s, base:base + tt, :] = xc
                xa_buf[s, base:base + SUBLANES, :] = xa_bs[tt - SUBLANES:]
                cn_ref[bi, b, :, ls] = xa_bs[tt - SUBLANES:]

    neg_lam = -lam_ref[...]
    softplus_neg_lam = jnp.maximum(neg_lam, 0.0) + jnp.log1p(jnp.exp(-jnp.abs(neg_lam)))
    log2_a_per_r = (-LRU_C * math.log2(math.e)) * softplus_neg_lam
    for j in range(N_GATE_GROUPS):
        s_lo = SLABS_PER_CHUNK * j
        cols = slice(j * MXU_N, (j + 1) * MXU_N)
        xc = jnp.concatenate([buf1[s_lo + s2] for s2 in range(SLABS_PER_CHUNK)], axis=1)
        pre = _dot(xc.astype(BF16), wg_ref[j])
        r = jax.nn.sigmoid(pre[:, :MXU_N] + bg_ref[0:1, cols])
        i_gate = jax.nn.sigmoid(pre[:, MXU_N:] + bg_ref[1:2, cols])
        a = jnp.exp2(r * log2_a_per_r[:, cols])
        y = 1.0 - a * a
        mult = y * lax.rsqrt(jnp.maximum(y, SQRT_GUARD))
        u = mult * (i_gate * xc)
        for s2 in range(SLABS_PER_CHUNK):
            buf1[s_lo + s2] = a[:, lane_tile(s2)]
            buf2[s_lo + s2] = u[:, lane_tile(s2)]
        gl_scr[:, cols] = _gelu(in_proj(D_RNN, j))
        gu_scr[:, cols] = _gelu(in_proj(2 * D_RNN, j))
        wide_scr[:, cols] = _gelu(in_proj(2 * D_RNN + D_GMLP, j))
        ga_scr[:, cols] = jax.nn.sigmoid(in_proj(2 * D_RNN + 2 * D_GMLP, j))
        gb_scr[:, cols] = jax.nn.sigmoid(in_proj(2 * D_RNN + 2 * D_GMLP + D_MODEL, j))

    hs = [hcar[:, lane_tile(s)] for s in range(N_SLABS)]
    for t in range(tt):
        for s in range(N_SLABS):
            hs[s] = buf1[s, seq_rows(t), :] * hs[s] + buf2[s, seq_rows(t), :]
            buf2[s, seq_rows(t), :] = hs[s]
    for s in range(N_SLABS):
        hcar[:, lane_tile(s)] = hs[s]
    hl_ref[bi] = hcar[...]

    for s in range(N_SLABS):
        for b in range(nb):
            ya_scr[b * tt:(b + 1) * tt, lane_tile(s)] = (
                buf2[s, b * pitch:b * pitch + tt, :]
                * gl_scr[b * tt:(b + 1) * tt, lane_tile(s)]).astype(BF16)

    lng = lng_ref[...]
    lnb = lnb_ref[...]
    for b, r0 in row_blocks:
        rows = slice(b * tt + r0, b * tt + r0 + ROW_BLOCK)
        gv = wide_scr[rows, :]
        d = gv - jnp.mean(gv, axis=-1, keepdims=True)
        vn = d * lax.rsqrt(jnp.mean(d * d, axis=-1, keepdims=True) + EPS) * lng + lnb
        vnb_scr[rows, :] = vn.astype(BF16)
        if emit_vn:
            vn_ref[b, r0:r0 + ROW_BLOCK, :] = vn

    pos = lax.broadcasted_iota(jnp.int32, (tt, tt), 0) // CHUNK
    qos = lax.broadcasted_iota(jnp.int32, (tt, tt), 1) // CHUNK
    causal = qos <= pos
    for g in range(N_GMLP_GROUPS):
        gcols = lane_tile(g)
        wm = jnp.where(causal, ws_ref[g], 0.0).astype(BF16)
        v_g = jnp.concatenate([vnb_scr[b * tt:(b + 1) * tt, gcols] for b in range(nb)], axis=1)
        sg = _dot(wm, v_g) + bst_ref[:, g:g + 1]
        for b in range(nb):
            yb_scr[b * tt:(b + 1) * tt, gcols] = (
                gu_scr[b * tt:(b + 1) * tt, gcols] * sg[:, lane_tile(b)]).astype(BF16)

    for c in range(D_MODEL // MXU_N):
        cols = slice(c * MXU_N, (c + 1) * MXU_N)
        o_a = _dot(ya_scr[...], wbra_ref[:, cols])
        o_b = _dot(yb_scr[...], wbrb_ref[:, cols])
        m_scr[:, cols] = (ga_scr[:, cols] * o_a + gb_scr[:, cols] * o_b).astype(BF16)

    for c in range(D_MODEL // MXU_N):
        cols = slice(c * MXU_N, (c + 1) * MXU_N)
        wide_scr[:, cols] = _dot(m_scr[...], wout_ref[:, cols])
    gpost = gpost_ref[...]
    for b, r0 in row_blocks:
        mix = wide_scr[b * tt + r0:b * tt + r0 + ROW_BLOCK, :]
        x1_ref[b, r0:r0 + ROW_BLOCK, :] = (
            x_ref[b, r0:r0 + ROW_BLOCK, :] + mix * _rms_scale(mix) * gpost)


def _ffn_kernel(x_ref, gpre_ref, w1_ref, w2_ref, gpost_ref, o_ref, hn_scr, f_scr, wide_scr, *, m):
    gpre = gpre_ref[...]
    for r0 in range(0, m, ROW_BLOCK):
        xb = x_ref[r0:r0 + ROW_BLOCK, :]
        hn_scr[r0:r0 + ROW_BLOCK, :] = (xb * _rms_scale(xb) * gpre).astype(BF16)

    for c in range(D_FF // MXU_N):
        gate = _dot(hn_scr[...], w1_ref[:, c * MXU_N:(c + 1) * MXU_N])
        up = _dot(hn_scr[...], w1_ref[:, D_FF + c * MXU_N:D_FF + (c + 1) * MXU_N])
        f_scr[:, c * MXU_N:(c + 1) * MXU_N] = (jax.nn.silu(gate) * up).astype(BF16)

    for c in range(D_MODEL // MXU_N):
        cols = slice(c * MXU_N, (c + 1) * MXU_N)
        wide_scr[:, cols] = _dot(f_scr[...], w2_ref[:, cols])
    gpost = gpost_ref[...]
    for r0 in range(0, m, ROW_BLOCK):
        f = wide_scr[r0:r0 + ROW_BLOCK, :]
        o_ref[r0:r0 + ROW_BLOCK, :] = x_ref[r0:r0 + ROW_BLOCK, :] + f * _rms_scale(f) * gpost


def _resident(shape):
    nd = len(shape)
    return pl.BlockSpec(shape, lambda *_: (0,) * nd, pipeline_mode=pl.Buffered(1))


def _mixer_call(x, h0, c0, p, *, nb, tt, emit_vn):
    bsz, t_len, _ = x.shape
    n_t = t_len // tt
    m = nb * tt
    pitch = tt + SUBLANES
    n_b = bsz // nb
    x4 = x.reshape(bsz, n_t, tt, D_MODEL)
    tile = pl.BlockSpec((nb, None, tt, D_MODEL), lambda bi, ti: (bi, ti, 0, 0))
    h0 = h0.reshape(n_b, nb, D_RNN)
    c0 = jnp.pad(c0, ((0, 0), (SUBLANES - (CONV_W - 1), 0), (0, 0))).reshape(n_b, nb, SUBLANES, D_RNN)
    small = [h0, c0, p["g_pre_mix"], p["w_in"], p["conv_w"], p["conv_b"], p["w_gate"], p["b_gate"],
             p["lam"], p["w_br_a"], p["ln_g"], p["ln_b"], p["w_s"][:, :tt, :tt], p["b_s_t"][:tt],
             p["w_br_b"], p["w_out"], p["g_post_mix"]]
    out_shape = [jax.ShapeDtypeStruct(x4.shape, F32),
                 jax.ShapeDtypeStruct(h0.shape, F32),
                 jax.ShapeDtypeStruct(c0.shape, F32)]
    out_specs = [tile,
                 pl.BlockSpec(h0.shape, lambda bi, ti: (0, 0, 0)),
                 pl.BlockSpec(c0.shape, lambda bi, ti: (0, 0, 0, 0))]
    if emit_vn:
        out_shape.append(jax.ShapeDtypeStruct(x4.shape, F32))
        out_specs.append(tile)
    scratch = [pltpu.VMEM((m, D_MODEL), BF16),
               pltpu.VMEM((N_SLABS, nb * pitch, LANES), F32),
               pltpu.VMEM((N_SLABS, nb * pitch, LANES), F32),
               pltpu.VMEM((m, D_MODEL), F32),
               pltpu.VMEM((m, D_RNN), F32),
               pltpu.VMEM((m, D_GMLP), F32),
               pltpu.VMEM((m, D_MODEL), F32),
               pltpu.VMEM((m, D_MODEL), F32),
               pltpu.VMEM((m, D_RNN), BF16),
               pltpu.VMEM((m, D_GMLP), BF16),
               pltpu.VMEM((m, D_GMLP), BF16),
               pltpu.VMEM((m, D_MODEL), BF16),
               pltpu.VMEM((nb, D_RNN), F32),
               pltpu.VMEM((N_SLABS, nb * pitch, LANES), F32)]
    outs = pl.pallas_call(
        functools.partial(_mixer_kernel, nb=nb, tt=tt, emit_vn=emit_vn),
        grid=(n_b, n_t),
        in_specs=[tile] + [_resident(a.shape) for a in small],
        out_specs=out_specs,
        out_shape=out_shape,
        scratch_shapes=scratch,
        compiler_params=pltpu.CompilerParams(
            dimension_semantics=("arbitrary", "arbitrary"), vmem_limit_bytes=VMEM_LIMIT_BYTES),
        name="mixer",
    )(x4, *small)
    x1 = outs[0].reshape(bsz * t_len, D_MODEL)
    vn = outs[3].reshape(bsz, t_len, D_GMLP) if emit_vn else None
    h_last = outs[1].reshape(bsz, D_RNN)
    conv_new = outs[2].reshape(bsz, SUBLANES, D_RNN)[:, SUBLANES - (CONV_W - 1):]
    return x1, h_last, conv_new, vn


def _ffn_call(x1, p, *, m):
    rows = x1.shape[0]
    tile = pl.BlockSpec((m, D_MODEL), lambda i: (i, 0))
    small = [p["g_pre_ffn"], p["w_ffn_in"], p["w_ffn_out"], p["g_post_ffn"]]
    return pl.pallas_call(
        functools.partial(_ffn_kernel, m=m),
        grid=(rows // m,),
        in_specs=[tile] + [_resident(a.shape) for a in small],
        out_specs=tile,
        out_shape=jax.ShapeDtypeStruct((rows, D_MODEL), F32),
        scratch_shapes=[pltpu.VMEM((m, D_MODEL), BF16),
                        pltpu.VMEM((m, D_FF), BF16),
                        pltpu.VMEM((m, D_MODEL), F32)],
        compiler_params=pltpu.CompilerParams(
            dimension_semantics=("arbitrary",), vmem_limit_bytes=VMEM_LIMIT_BYTES),
        name="ffn",
    )(x1, *small)


def _block_diag_gates(w_a, w_x):
    eye = jnp.eye(HEADS_PER_MXU, dtype=w_a.dtype)

    def bd(w):
        w = w.reshape(N_GATE_GROUPS, HEADS_PER_MXU, RNN_HEAD_DIM, RNN_HEAD_DIM)
        return jnp.einsum("ghij,hk->ghikj", w, eye).reshape(N_GATE_GROUPS, MXU_N, MXU_N)

    return jnp.concatenate([bd(w_a), bd(w_x)], axis=-1)


def _mxu_weight(w):
    w = w.astype(BF16)
    if (w.shape[-1] // LANES) % SUBLANES == 0:
        w = jnp.pad(w, [(0, 0)] * (w.ndim - 1) + [(0, LANES)])
    return w


def _layer_params(l, g_pre_mix, w_in, conv_w, conv_b, w_a, b_a, w_x, b_x, lam, w_br_a, ln_g, ln_b,
                  w_s, b_s, w_br_b, w_out, g_post_mix, g_pre_ffn, w_ffn_in, w_ffn_out, g_post_ffn):
    row = lambda v: v[l].reshape(1, -1)
    return {
        "g_pre_mix": row(g_pre_mix), "w_in": _mxu_weight(w_in[l]),
        "conv_w": conv_w[l], "conv_b": row(conv_b),
        "w_gate": _mxu_weight(_block_diag_gates(w_a[l], w_x[l])),
        "b_gate": jnp.stack([b_a[l].reshape(-1), b_x[l].reshape(-1)]),
        "lam": row(lam), "w_br_a": _mxu_weight(w_br_a[l]),
        "ln_g": row(ln_g), "ln_b": row(ln_b),
        "w_s": w_s[l], "b_s_t": b_s[l].T,
        "w_br_b": _mxu_weight(w_br_b[l]), "w_out": _mxu_weight(w_out[l]),
        "g_post_mix": row(g_post_mix), "g_pre_ffn": row(g_pre_ffn),
        "w_ffn_in": _mxu_weight(w_ffn_in[l]), "w_ffn_out": _mxu_weight(w_ffn_out[l]),
        "g_post_ffn": row(g_post_ffn),
    }


def kernel(x_prompt, x_sample, state_rglru_h, state_rglru_conv, g_pre_mix, w_in, conv_w, conv_b, w_a, b_a, w_x, b_x, lam, w_br_a, ln_g, ln_b, w_s, b_s, w_br_b, w_out, g_post_mix, g_pre_ffn, w_ffn_in, w_ffn_out, g_post_ffn):
    depth = w_in.shape[0]
    bp, tp, _ = x_prompt.shape
    bs, ts, _ = x_sample.shape
    xp, xs = x_prompt, x_sample
    hp_l, cp_l, hs_l, cs_l, vs_l = [], [], [], [], []
    for l in range(depth):
        p = _layer_params(l, g_pre_mix, w_in, conv_w, conv_b, w_a, b_a, w_x, b_x, lam, w_br_a, ln_g,
                          ln_b, w_s, b_s, w_br_b, w_out, g_post_mix, g_pre_ffn, w_ffn_in, w_ffn_out,
                          g_post_ffn)
        h0_p = jnp.zeros((bp, D_RNN), F32)
        c0_p = jnp.zeros((bp, CONV_W - 1, D_RNN), F32)
        x1p, hp, cp, _ = _mixer_call(xp, h0_p, c0_p, p, nb=4, tt=MLP_CHUNK, emit_vn=False)
        xp = _ffn_call(x1p, p, m=512).reshape(bp, tp, D_MODEL)
        x1s, hs, cs, vs = _mixer_call(xs, state_rglru_h[l], state_rglru_conv[l], p, nb=bs, tt=ts,
                                      emit_vn=True)
        xs = _ffn_call(x1s, p, m=bs * ts).reshape(bs, ts, D_MODEL)
        hp_l.append(hp)
        cp_l.append(cp)
        hs_l.append(hs)
        cs_l.append(cs)
        vs_l.append(vs)
    return (xp, xs, jnp.stack(hp_l), jnp.stack(cp_l), jnp.stack(hs_l), jnp.stack(cs_l),
            jnp.stack(vs_l))
```

```python
import functools
import math

import jax
import jax.numpy as jnp
from jax import lax
from jax.experimental import pallas as pl
from jax.experimental.pallas import tpu as pltpu

D_MODEL = 1024
D_RNN = 1024
N_RNN_HEADS = 16
RNN_HEAD_DIM = D_RNN // N_RNN_HEADS
CONV_W = 4
LRU_C = 8.0
D_GMLP = 1024
N_GMLP_GROUPS = 8
GMLP_GROUP_DIM = D_GMLP // N_GMLP_GROUPS
CHUNK = 64
MLP_CHUNK = 128
D_FF = 2816
EPS = 1e-6
SQRT_GUARD = 1e-30

LANES = 128
SUBLANES = 8
MXU_N = 256
SCAN_PAD = 4
N_SLABS = D_RNN // LANES
SLABS_PER_CHUNK = MXU_N // LANES
HEADS_PER_MXU = MXU_N // RNN_HEAD_DIM
N_GATE_GROUPS = D_RNN // MXU_N
ROW_BLOCK = 32
VMEM_LIMIT_BYTES = 56 * 1024 * 1024

BF16 = jnp.bfloat16
F32 = jnp.float32


def _dot(a, b):
    return jnp.dot(a, b, preferred_element_type=F32)


_GELU_K1 = -2.0 * math.sqrt(2.0 / math.pi) * math.log2(math.e)
_GELU_K2 = 0.044715 * _GELU_K1


def _gelu(x):
    return x / (1.0 + jnp.exp2(x * (_GELU_K1 + _GELU_K2 * (x * x))))


def _rms_scale(v):
    return lax.rsqrt(jnp.mean(v * v, axis=-1, keepdims=True) + EPS)


def _mixer_kernel(x_ref, h0_ref, c0_ref, gpre_ref, win_ref, cw_ref, cb_ref, wg_ref, bg_ref, lam_ref,
                  wbra_ref, lng_ref, lnb_ref, ws_ref, bst_ref, wbrb_ref, wout_ref, gpost_ref,
                  *rest, nb, tt, emit_vn):
    if emit_vn:
        x1_ref, hl_ref, cn_ref, vn_ref = rest[:4]
        scratch = rest[4:]
    else:
        x1_ref, hl_ref, cn_ref = rest[:3]
        vn_ref = None
        scratch = rest[3:]
    (xn_scr, buf1, buf2, wide_scr, gl_scr, gu_scr, ga_scr, gb_scr, ya_scr, yb_scr, vhist, m_scr,
     hcar, xa_buf) = scratch

    pitch = tt + SCAN_PAD
    xa_pitch = tt + SUBLANES
    chunk_len = ws_ref.shape[1]
    bi = pl.program_id(0)
    ti = pl.program_id(1)
    sub0 = 0 if chunk_len == tt else pl.multiple_of((ti % (chunk_len // tt)) * tt, tt)
    row_blocks = [(b, r0) for b in range(nb) for r0 in range(0, tt, ROW_BLOCK)]

    def seq_rows(t):
        return pl.ds(t, nb, stride=pitch)

    def lane_tile(s):
        return slice(s * LANES, (s + 1) * LANES)

    @pl.when(ti == 0)
    def _():
        hcar[...] = h0_ref[bi]
        for s in range(N_SLABS):
            for b in range(nb):
                xa_buf[s, b * xa_pitch:b * xa_pitch + SUBLANES, :] = c0_ref[bi, b, :, lane_tile(s)]
        vhist[...] = jnp.zeros(vhist.shape, BF16)

    for s in range(N_SLABS):
        for b in range(nb):
            buf1[s, b * pitch + tt:(b + 1) * pitch, :] = jnp.zeros((SCAN_PAD, LANES), F32)

    gpre = gpre_ref[...]
    for b, r0 in row_blocks:
        xb = x_ref[b, r0:r0 + ROW_BLOCK, :]
        xn_scr[b * tt + r0:b * tt + r0 + ROW_BLOCK, :] = (xb * _rms_scale(xb) * gpre).astype(BF16)

    def in_proj(col0, c):
        lo = col0 + c * MXU_N
        return _dot(xn_scr[...], win_ref[:, lo:lo + MXU_N])

    for c in range(D_RNN // MXU_N):
        xa = in_proj(0, c)
        for s2 in range(SLABS_PER_CHUNK):
            s = SLABS_PER_CHUNK * c + s2
            ls = lane_tile(s)
            w = [cw_ref[k:k + 1, ls] for k in range(CONV_W)]
            bias = cb_ref[:, ls]
            for b in range(nb):
                base = b * xa_pitch
                xa_bs = xa[b * tt:(b + 1) * tt, lane_tile(s2)]
                xa_buf[s, base + SUBLANES:base + xa_pitch, :] = xa_bs
                xc = bias + w[CONV_W - 1] * xa_bs
                for k in range(CONV_W - 1):
                    lo = base + SUBLANES - (CONV_W - 1) + k
                    xc = xc + w[k] * xa_buf[s, lo:lo + tt, :]
                buf1[s, b * pitch:b * pitch + tt, :] = xc
                xa_buf[s, base:base + SUBLANES, :] = xa_bs[tt - SUBLANES:]
                cn_ref[bi, b, :, ls] = xa_bs[tt - SUBLANES:]

    neg_lam = -lam_ref[...]
    softplus_neg_lam = jnp.maximum(neg_lam, 0.0) + jnp.log1p(jnp.exp(-jnp.abs(neg_lam)))
    log2_a_per_r = (-LRU_C * math.log2(math.e)) * softplus_neg_lam
    for j in range(N_GATE_GROUPS):
        s_lo = SLABS_PER_CHUNK * j
        cols = slice(j * MXU_N, (j + 1) * MXU_N)
        xc = jnp.concatenate([buf1[s_lo + s2] for s2 in range(SLABS_PER_CHUNK)], axis=1)
        pre = _dot(xc.astype(BF16), wg_ref[j])
        r = jax.nn.sigmoid(pre[:, :MXU_N] + bg_ref[0:1, cols])
        i_gate = jax.nn.sigmoid(pre[:, MXU_N:] + bg_ref[1:2, cols])
        a = jnp.exp2(r * log2_a_per_r[:, cols])
        y = 1.0 - a * a
        mult = y * lax.rsqrt(jnp.maximum(y, SQRT_GUARD))
        u = mult * (i_gate * xc)
        for s2 in range(SLABS_PER_CHUNK):
            buf1[s_lo + s2] = a[:, lane_tile(s2)]
            buf2[s_lo + s2] = u[:, lane_tile(s2)]
        gl_scr[:, cols] = _gelu(in_proj(D_RNN, j))
        gu_scr[:, cols] = _gelu(in_proj(2 * D_RNN, j))
        wide_scr[:, cols] = _gelu(in_proj(2 * D_RNN + D_GMLP, j))
        ga_scr[:, cols] = jax.nn.sigmoid(in_proj(2 * D_RNN + 2 * D_GMLP, j))
        gb_scr[:, cols] = jax.nn.sigmoid(in_proj(2 * D_RNN + 2 * D_GMLP + D_MODEL, j))

    hs = [hcar[:, lane_tile(s)] for s in range(N_SLABS)]
    for t in range(tt):
        for s in range(N_SLABS):
            hs[s] = buf1[s, seq_rows(t), :] * hs[s] + buf2[s, seq_rows(t), :]
            buf2[s, seq_rows(t), :] = hs[s]
    for s in range(N_SLABS):
        hcar[:, lane_tile(s)] = hs[s]
    hl_ref[bi] = hcar[...]

    for s in range(N_SLABS):
        for b in range(nb):
            ya_scr[b * tt:(b + 1) * tt, lane_tile(s)] = (
                buf2[s, b * pitch:b * pitch + tt, :]
                * gl_scr[b * tt:(b + 1) * tt, lane_tile(s)]).astype(BF16)

    lng = lng_ref[...]
    lnb = lnb_ref[...]
    for b, r0 in row_blocks:
        rows = slice(b * tt + r0, b * tt + r0 + ROW_BLOCK)
        gv = wide_scr[rows, :]
        d = gv - jnp.mean(gv, axis=-1, keepdims=True)
        vn = d * lax.rsqrt(jnp.mean(d * d, axis=-1, keepdims=True) + EPS) * lng + lnb
        vhist[pl.ds(b * chunk_len + sub0 + r0, ROW_BLOCK), :] = vn.astype(BF16)
        if emit_vn:
            vn_ref[b, r0:r0 + ROW_BLOCK, :] = vn

    pos = (lax.broadcasted_iota(jnp.int32, (tt, chunk_len), 0) + sub0) // CHUNK
    qos = lax.broadcasted_iota(jnp.int32, (tt, chunk_len), 1) // CHUNK
    causal = qos <= pos
    for g in range(N_GMLP_GROUPS):
        gcols = lane_tile(g)
        wm = jnp.where(causal, ws_ref[g, pl.ds(sub0, tt), :], 0.0).astype(BF16)
        v_g = jnp.concatenate(
            [vhist[b * chunk_len:(b + 1) * chunk_len, gcols] for b in range(nb)], axis=1)
        sg = _dot(wm, v_g) + bst_ref[pl.ds(sub0, tt), g:g + 1]
        for b in range(nb):
            yb_scr[b * tt:(b + 1) * tt, gcols] = (
                gu_scr[b * tt:(b + 1) * tt, gcols] * sg[:, lane_tile(b)]).astype(BF16)

    for c in range(D_MODEL // MXU_N):
        cols = slice(c * MXU_N, (c + 1) * MXU_N)
        o_a = _dot(ya_scr[...], wbra_ref[:, cols])
        o_b = _dot(yb_scr[...], wbrb_ref[:, cols])
        m_scr[:, cols] = (ga_scr[:, cols] * o_a + gb_scr[:, cols] * o_b).astype(BF16)

    for c in range(D_MODEL // MXU_N):
        cols = slice(c * MXU_N, (c + 1) * MXU_N)
        wide_scr[:, cols] = _dot(m_scr[...], wout_ref[:, cols])
    gpost = gpost_ref[...]
    for b, r0 in row_blocks:
        mix = wide_scr[b * tt + r0:b * tt + r0 + ROW_BLOCK, :]
        x1_ref[b, r0:r0 + ROW_BLOCK, :] = (
            x_ref[b, r0:r0 + ROW_BLOCK, :] + mix * _rms_scale(mix) * gpost)


def _ffn_kernel(x_ref, gpre_ref, w1_ref, w2_ref, gpost_ref, o_ref, hn_scr, f_scr, wide_scr, *, m):
    gpre = gpre_ref[...]
    for r0 in range(0, m, ROW_BLOCK):
        xb = x_ref[r0:r0 + ROW_BLOCK, :]
        hn_scr[r0:r0 + ROW_BLOCK, :] = (xb * _rms_scale(xb) * gpre).astype(BF16)

    for c in range(D_FF // MXU_N):
        gate = _dot(hn_scr[...], w1_ref[:, c * MXU_N:(c + 1) * MXU_N])
        up = _dot(hn_scr[...], w1_ref[:, D_FF + c * MXU_N:D_FF + (c + 1) * MXU_N])
        f_scr[:, c * MXU_N:(c + 1) * MXU_N] = (jax.nn.silu(gate) * up).astype(BF16)

    for c in range(D_MODEL // MXU_N):
        cols = slice(c * MXU_N, (c + 1) * MXU_N)
        wide_scr[:, cols] = _dot(f_scr[...], w2_ref[:, cols])
    gpost = gpost_ref[...]
    for r0 in range(0, m, ROW_BLOCK):
        f = wide_scr[r0:r0 + ROW_BLOCK, :]
        o_ref[r0:r0 + ROW_BLOCK, :] = x_ref[r0:r0 + ROW_BLOCK, :] + f * _rms_scale(f) * gpost


def _resident(shape):
    nd = len(shape)
    return pl.BlockSpec(shape, lambda *_: (0,) * nd, pipeline_mode=pl.Buffered(1))


def _mixer_call(x, h0, c0, p, *, nb, tt, emit_vn):
    bsz, t_len, _ = x.shape
    n_t = t_len // tt
    m = nb * tt
    pitch = tt + SCAN_PAD
    chunk_len = min(t_len, MLP_CHUNK)
    assert chunk_len == tt or (tt % CHUNK == 0 and chunk_len % tt == 0)
    n_b = bsz // nb
    x4 = x.reshape(bsz, n_t, tt, D_MODEL)
    tile = pl.BlockSpec((nb, None, tt, D_MODEL), lambda bi, ti: (bi, ti, 0, 0))
    h0 = h0.reshape(n_b, nb, D_RNN)
    c0 = jnp.pad(c0, ((0, 0), (SUBLANES - (CONV_W - 1), 0), (0, 0))).reshape(n_b, nb, SUBLANES, D_RNN)
    small = [h0, c0, p["g_pre_mix"], p["w_in"], p["conv_w"], p["conv_b"], p["w_gate"], p["b_gate"],
             p["lam"], p["w_br_a"], p["ln_g"], p["ln_b"], p["w_s"][:, :chunk_len, :chunk_len],
             p["b_s_t"][:chunk_len],
             p["w_br_b"], p["w_out"], p["g_post_mix"]]
    out_shape = [jax.ShapeDtypeStruct(x4.shape, F32),
                 jax.ShapeDtypeStruct(h0.shape, F32),
                 jax.ShapeDtypeStruct(c0.shape, F32)]
    out_specs = [tile,
                 pl.BlockSpec(h0.shape, lambda bi, ti: (0, 0, 0)),
                 pl.BlockSpec(c0.shape, lambda bi, ti: (0, 0, 0, 0))]
    if emit_vn:
        out_shape.append(jax.ShapeDtypeStruct(x4.shape, F32))
        out_specs.append(tile)
    scratch = [pltpu.VMEM((m, D_MODEL), BF16),
               pltpu.VMEM((N_SLABS, nb * pitch, LANES), F32),
               pltpu.VMEM((N_SLABS, nb * pitch, LANES), F32),
               pltpu.VMEM((m, D_MODEL), F32),
               pltpu.VMEM((m, D_RNN), F32),
               pltpu.VMEM((m, D_GMLP), F32),
               pltpu.VMEM((m, D_MODEL), F32),
               pltpu.VMEM((m, D_MODEL), F32),
               pltpu.VMEM((m, D_RNN), BF16),
               pltpu.VMEM((m, D_GMLP), BF16),
               pltpu.VMEM((nb * chunk_len, D_GMLP), BF16),
               pltpu.VMEM((m, D_MODEL), BF16),
               pltpu.VMEM((nb, D_RNN), F32),
               pltpu.VMEM((N_SLABS, nb * (tt + SUBLANES), LANES), F32)]
    outs = pl.pallas_call(
        functools.partial(_mixer_kernel, nb=nb, tt=tt, emit_vn=emit_vn),
        grid=(n_b, n_t),
        in_specs=[tile] + [_resident(a.shape) for a in small],
        out_specs=out_specs,
        out_shape=out_shape,
        scratch_shapes=scratch,
        compiler_params=pltpu.CompilerParams(
            dimension_semantics=("arbitrary", "arbitrary"), vmem_limit_bytes=VMEM_LIMIT_BYTES),
        name="mixer",
    )(x4, *small)
    x1 = outs[0].reshape(bsz * t_len, D_MODEL)
    vn = outs[3].reshape(bsz, t_len, D_GMLP) if emit_vn else None
    h_last = outs[1].reshape(bsz, D_RNN)
    conv_new = outs[2].reshape(bsz, SUBLANES, D_RNN)[:, SUBLANES - (CONV_W - 1):]
    return x1, h_last, conv_new, vn


def _ffn_call(x1, p, *, m):
    rows = x1.shape[0]
    tile = pl.BlockSpec((m, D_MODEL), lambda i: (i, 0))
    small = [p["g_pre_ffn"], p["w_ffn_in"], p["w_ffn_out"], p["g_post_ffn"]]
    return pl.pallas_call(
        functools.partial(_ffn_kernel, m=m),
        grid=(rows // m,),
        in_specs=[tile] + [_resident(a.shape) for a in small],
        out_specs=tile,
        out_shape=jax.ShapeDtypeStruct((rows, D_MODEL), F32),
        scratch_shapes=[pltpu.VMEM((m, D_MODEL), BF16),
                        pltpu.VMEM((m, D_FF), BF16),
                        pltpu.VMEM((m, D_MODEL), F32)],
        compiler_params=pltpu.CompilerParams(
            dimension_semantics=("arbitrary",), vmem_limit_bytes=VMEM_LIMIT_BYTES),
        name="ffn",
    )(x1, *small)


def _block_diag_gates(w_a, w_x):
    eye = jnp.eye(HEADS_PER_MXU, dtype=w_a.dtype)

    def bd(w):
        w = w.reshape(N_GATE_GROUPS, HEADS_PER_MXU, RNN_HEAD_DIM, RNN_HEAD_DIM)
        return jnp.einsum("ghij,hk->ghikj", w, eye).reshape(N_GATE_GROUPS, MXU_N, MXU_N)

    return jnp.concatenate([bd(w_a), bd(w_x)], axis=-1)


def _mxu_weight(w):
    w = w.astype(BF16)
    if (w.shape[-1] // LANES) % SUBLANES == 0:
        w = jnp.concatenate([w, jnp.zeros(w.shape[:-1] + (LANES,), BF16)], axis=-1)
    return w


def _layer_params(l, g_pre_mix, w_in, conv_w, conv_b, w_a, b_a, w_x, b_x, lam, w_br_a, ln_g, ln_b,
                  w_s, b_s, w_br_b, w_out, g_post_mix, g_pre_ffn, w_ffn_in, w_ffn_out, g_post_ffn):
    row = lambda v: v[l].reshape(1, -1)
    return {
        "g_pre_mix": row(g_pre_mix), "w_in": _mxu_weight(w_in[l]),
        "conv_w": conv_w[l], "conv_b": row(conv_b),
        "w_gate": _mxu_weight(_block_diag_gates(w_a[l], w_x[l])),
        "b_gate": jnp.stack([b_a[l].reshape(-1), b_x[l].reshape(-1)]),
        "lam": row(lam), "w_br_a": _mxu_weight(w_br_a[l]),
        "ln_g": row(ln_g), "ln_b": row(ln_b),
        "w_s": w_s[l], "b_s_t": b_s[l].T,
        "w_br_b": _mxu_weight(w_br_b[l]), "w_out": _mxu_weight(w_out[l]),
        "g_post_mix": row(g_post_mix), "g_pre_ffn": row(g_pre_ffn),
        "w_ffn_in": _mxu_weight(w_ffn_in[l]), "w_ffn_out": _mxu_weight(w_ffn_out[l]),
        "g_post_ffn": row(g_post_ffn),
    }


def kernel(x_prompt, x_sample, state_rglru_h, state_rglru_conv, g_pre_mix, w_in, conv_w, conv_b, w_a, b_a, w_x, b_x, lam, w_br_a, ln_g, ln_b, w_s, b_s, w_br_b, w_out, g_post_mix, g_pre_ffn, w_ffn_in, w_ffn_out, g_post_ffn):
    depth = w_in.shape[0]
    bp, tp, _ = x_prompt.shape
    bs, ts, _ = x_sample.shape
    xp, xs = x_prompt, x_sample
    hp_l, cp_l, hs_l, cs_l, vs_l = [], [], [], [], []
    for l in range(depth):
        p = _layer_params(l, g_pre_mix, w_in, conv_w, conv_b, w_a, b_a, w_x, b_x, lam, w_br_a, ln_g,
                          ln_b, w_s, b_s, w_br_b, w_out, g_post_mix, g_pre_ffn, w_ffn_in, w_ffn_out,
                          g_post_ffn)
        h0_p = jnp.zeros((bp, D_RNN), F32)
        c0_p = jnp.zeros((bp, CONV_W - 1, D_RNN), F32)
        x1p, hp, cp, _ = _mixer_call(xp, h0_p, c0_p, p, nb=SUBLANES, tt=CHUNK, emit_vn=False)
        xp = _ffn_call(x1p, p, m=512).reshape(bp, tp, D_MODEL)
        x1s, hs, cs, vs = _mixer_call(xs, state_rglru_h[l], state_rglru_conv[l], p, nb=bs, tt=ts,
                                      emit_vn=True)
        xs = _ffn_call(x1s, p, m=bs * ts).reshape(bs, ts, D_MODEL)
        hp_l.append(hp)
        cp_l.append(cp)
        hs_l.append(hs)
        cs_l.append(cs)
        vs_l.append(vs)
    return (xp, xs, jnp.stack(hp_l), jnp.stack(cp_l), jnp.stack(hs_l), jnp.stack(cs_l),
            jnp.stack(vs_l))
```
